```python
import math
import jax, jax.numpy as jnp
from jax import lax
import numpy as np

D_MODEL = 1024
BATCH = 4
SEQ = 8192
DEPTH = 1

MEM_LEN = 256
MLA_HEADS = 4
QK_NOPE_DIM = 128
QK_ROPE_DIM = 64
QK_DIM = QK_NOPE_DIM + QK_ROPE_DIM
V_HEAD_DIM = 128
Q_LORA_RANK = 384
KV_LORA_RANK = 256
MLA_WIDTH = MLA_HEADS * V_HEAD_DIM
ROPE_THETA = 10000.0
Q_BLOCK = 128
LRU_WIDTH = D_MODEL - MLA_WIDTH
LRU_BLOCKS = 8
LRU_BLOCK_DIM = LRU_WIDTH // LRU_BLOCKS
CONV_WIDTH = 4
LRU_C = 8.0
IN_COLS = Q_LORA_RANK + KV_LORA_RANK + QK_ROPE_DIM + 2 * LRU_WIDTH
XA_HEADS = 4
XA_HEAD_DIM = 128
XA_WIDTH = XA_HEADS * XA_HEAD_DIM
D_FF = 2816
EPS = 1e-6
NEG_INF = -1e30

kernel_name = "hymba_mla_rglru_macaron_sandwich"


def rmsnorm(x, g):
    xf = x.astype(jnp.float32)
    y = xf * lax.rsqrt(jnp.mean(xf * xf, axis=-1, keepdims=True) + EPS)
    return (y * g.astype(jnp.float32)).astype(x.dtype)


def swiglu(h, w_gu, w_down):
    g, u = jnp.split(h @ w_gu, 2, axis=-1)
    return (jax.nn.silu(g) * u) @ w_down


def rope_tables(positions):
    inv = ROPE_THETA ** (-jnp.arange(0, QK_ROPE_DIM, 2, dtype=jnp.float32) / QK_ROPE_DIM)
    ang = positions.astype(jnp.float32)[..., None] * inv
    return jnp.cos(ang), jnp.sin(ang)


def apply_rope(x, cos, sin):
    xf = x.astype(jnp.float32)
    x1, x2 = jnp.split(xf, 2, axis=-1)
    return jnp.concatenate([x1 * cos - x2 * sin, x2 * cos + x1 * sin], axis=-1).astype(x.dtype)


def mla(c_q, c_kv, k_pe, q_a_g, w_uq, kv_a_g, w_ukv, cos, sin):
    B, S, _ = c_q.shape
    q = (rmsnorm(c_q, q_a_g) @ w_uq).reshape(B, S, MLA_HEADS, QK_DIM)
    q_nope, q_pe = q[..., :QK_NOPE_DIM], q[..., QK_NOPE_DIM:]
    q_pe = apply_rope(q_pe, cos[:, :, None, :], sin[:, :, None, :])
    kv = (rmsnorm(c_kv, kv_a_g) @ w_ukv).reshape(B, S, MLA_HEADS, QK_NOPE_DIM + V_HEAD_DIM)
    k_nope, v = kv[..., :QK_NOPE_DIM], kv[..., QK_NOPE_DIM:]
    k_pe = apply_rope(k_pe, cos, sin)
    q = jnp.concatenate([q_nope, q_pe], axis=-1)
    k = jnp.concatenate(
        [k_nope, jnp.broadcast_to(k_pe[:, :, None, :], (B, S, MLA_HEADS, QK_ROPE_DIM))], axis=-1)
    scale = 1.0 / math.sqrt(QK_DIM)
    nb = S // Q_BLOCK
    q_blocks = q.reshape(B, nb, Q_BLOCK, MLA_HEADS, QK_DIM).transpose(1, 0, 2, 3, 4)
    key_pos = jnp.arange(S)

    def one_block(args):
        q_blk, i = args
        s = jnp.einsum('bqhd,bkhd->bhqk', q_blk, k,
                       preferred_element_type=jnp.float32) * scale
        q_pos = i * Q_BLOCK + jnp.arange(Q_BLOCK)
        s = jnp.where(key_pos[None, :] <= q_pos[:, None], s, NEG_INF)
        p = jax.nn.softmax(s, axis=-1).astype(v.dtype)
        return jnp.einsum('bhqk,bkhd->bqhd', p, v)

    o = lax.map(one_block, (q_blocks, jnp.arange(nb)))
    return o.transpose(1, 0, 2, 3, 4).reshape(B, S, MLA_WIDTH)


def rglru_group(u, gate, conv_w, conv_b, w_a, b_a, w_x, b_x, lam):
    B, S, W = u.shape
    u_pad = jnp.pad(u, ((0, 0), (CONV_WIDTH - 1, 0), (0, 0)))
    xc = conv_b
    for tap in range(CONV_WIDTH):
        xc = xc + u_pad[:, tap:tap + S] * conv_w[tap]
    xb = xc.reshape(B, S, LRU_BLOCKS, LRU_BLOCK_DIM)
    r = jax.nn.sigmoid(jnp.einsum('bsnd,nde->bsne', xb, w_a) + b_a).reshape(B, S, W)
    i = jax.nn.sigmoid(jnp.einsum('bsnd,nde->bsne', xb, w_x) + b_x).reshape(B, S, W)
    log_a = -LRU_C * r.astype(jnp.float32) * jax.nn.softplus(-lam.astype(jnp.float32))
    a = jnp.exp(log_a)
    b = jnp.sqrt(-jnp.expm1(2.0 * log_a)) * (i * xc).astype(jnp.float32)

    def combine(lhs, rhs):
        a1, b1 = lhs
        a2, b2 = rhs
        return a1 * a2, a2 * b1 + b2

    _, h = lax.associative_scan(combine, (a, b), axis=1)
    return h.astype(u.dtype) * jax.nn.gelu(gate)


def memory_xattn(h, mem_n, w_q, w_kv, w_o):
    B, S, _ = h.shape
    q = (h @ w_q).reshape(B, S, XA_HEADS, XA_HEAD_DIM)
    k, v = jnp.split(mem_n @ w_kv, 2, axis=-1)
    k = k.reshape(B, MEM_LEN, XA_HEADS, XA_HEAD_DIM)
    v = v.reshape(B, MEM_LEN, XA_HEADS, XA_HEAD_DIM)
    s = jnp.einsum('bshd,bmhd->bhsm', q, k,
                   preferred_element_type=jnp.float32) / math.sqrt(XA_HEAD_DIM)
    p = jax.nn.softmax(s, axis=-1).astype(v.dtype)
    o = jnp.einsum('bhsm,bmhd->bshd', p, v).reshape(B, S, XA_WIDTH)
    return o @ w_o


def setup_inputs(seed: int = 0) -> dict:
    key = jax.random.key(seed)
    ks = iter(jax.random.split(key, 40))
    f32 = jnp.float32

    def w(shape, fan_in):
        return jax.random.normal(next(ks), (DEPTH,) + shape, f32) * fan_in ** -0.5

    def gain(n):
        return 1.0 + 0.1 * jax.random.normal(next(ks), (DEPTH, n), f32)

    def bias(shape):
        return 0.01 * jax.random.normal(next(ks), (DEPTH,) + shape, f32)

    x = jax.random.normal(next(ks), (BATCH, SEQ, D_MODEL), f32)
    mem = jax.random.normal(next(ks), (BATCH, MEM_LEN, D_MODEL), f32)
    offset = jax.random.randint(next(ks), (BATCH, 1), 0, 1024, dtype=jnp.int32)
    positions = (jnp.arange(SEQ, dtype=jnp.int32)[None, :] + offset).astype(jnp.int32)
    u = jax.random.uniform(next(ks), (DEPTH, LRU_WIDTH), f32, 0.9, 0.999) ** (1.0 / LRU_C)
    rg_lambda = jnp.log(u) - jnp.log1p(-u)
    return {
        "x": x,
        "mem": mem,
        "positions": positions,
        "ffn1_pre_g": gain(D_MODEL),
        "ffn1_w_gu": w((D_MODEL, 2 * D_FF), D_MODEL),
        "ffn1_w_down": w((D_FF, D_MODEL), D_FF),
        "ffn1_post_g": gain(D_MODEL),
        "mix_pre_g": gain(D_MODEL),
        "w_in": w((D_MODEL, IN_COLS), D_MODEL),
        "q_a_norm_g": gain(Q_LORA_RANK),
        "w_uq": w((Q_LORA_RANK, MLA_HEADS * QK_DIM), Q_LORA_RANK),
        "kv_a_norm_g": gain(KV_LORA_RANK),
        "w_ukv": w((KV_LORA_RANK, MLA_HEADS * (QK_NOPE_DIM + V_HEAD_DIM)), KV_LORA_RANK),
        "conv_w": w((CONV_WIDTH, LRU_WIDTH), CONV_WIDTH),
        "conv_b": bias((LRU_WIDTH,)),
        "rg_w_a": w((LRU_BLOCKS, LRU_BLOCK_DIM, LRU_BLOCK_DIM), LRU_BLOCK_DIM),
        "rg_b_a": bias((LRU_BLOCKS, LRU_BLOCK_DIM)),
        "rg_w_x": w((LRU_BLOCKS, LRU_BLOCK_DIM, LRU_BLOCK_DIM), LRU_BLOCK_DIM),
        "rg_b_x": bias((LRU_BLOCKS, LRU_BLOCK_DIM)),
        "rg_lambda": rg_lambda,
        "w_out": w((D_MODEL, D_MODEL), D_MODEL),
        "mix_post_g": gain(D_MODEL),
        "xa_pre_g": gain(D_MODEL),
        "mem_norm_g": gain(D_MODEL),
        "xa_w_q": w((D_MODEL, XA_WIDTH), D_MODEL),
        "xa_w_kv": w((D_MODEL, 2 * XA_WIDTH), D_MODEL),
        "xa_w_o": w((XA_WIDTH, D_MODEL), XA_WIDTH),
        "xa_post_g": gain(D_MODEL),
        "ffn2_pre_g": gain(D_MODEL),
        "ffn2_w_gu": w((D_MODEL, 2 * D_FF), D_MODEL),
        "ffn2_w_down": w((D_FF, D_MODEL), D_FF),
        "ffn2_post_g": gain(D_MODEL),
    }


def reference(x, mem, positions, ffn1_pre_g, ffn1_w_gu, ffn1_w_down, ffn1_post_g,
              mix_pre_g, w_in, q_a_norm_g, w_uq, kv_a_norm_g, w_ukv, conv_w, conv_b,
              rg_w_a, rg_b_a, rg_w_x, rg_b_x, rg_lambda, w_out, mix_post_g,
              xa_pre_g, mem_norm_g, xa_w_q, xa_w_kv, xa_w_o, xa_post_g,
              ffn2_pre_g, ffn2_w_gu, ffn2_w_down, ffn2_post_g):
    cos, sin = rope_tables(positions)
    o1 = Q_LORA_RANK
    o2 = o1 + KV_LORA_RANK
    o3 = o2 + QK_ROPE_DIM
    o4 = o3 + LRU_WIDTH
    for l in range(DEPTH):
        h = rmsnorm(x, ffn1_pre_g[l])
        x = x + 0.5 * rmsnorm(swiglu(h, ffn1_w_gu[l], ffn1_w_down[l]), ffn1_post_g[l])

        h = rmsnorm(x, mix_pre_g[l])
        z = h @ w_in[l]
        c_q, c_kv, k_pe = z[..., :o1], z[..., o1:o2], z[..., o2:o3]
        u, gate = z[..., o3:o4], z[..., o4:]
        y_mla = mla(c_q, c_kv, k_pe, q_a_norm_g[l], w_uq[l], kv_a_norm_g[l], w_ukv[l], cos, sin)
        y_lru = rglru_group(u, gate, conv_w[l], conv_b[l], rg_w_a[l], rg_b_a[l],
                            rg_w_x[l], rg_b_x[l], rg_lambda[l])
        y = jnp.concatenate([y_mla, y_lru], axis=-1) @ w_out[l]
        x = x + rmsnorm(y, mix_post_g[l])

        h = rmsnorm(x, xa_pre_g[l])
        mem_n = rmsnorm(mem, mem_norm_g[l])
        y = memory_xattn(h, mem_n, xa_w_q[l], xa_w_kv[l], xa_w_o[l])
        x = x + rmsnorm(y, xa_post_g[l])

        h = rmsnorm(x, ffn2_pre_g[l])
        x = x + 0.5 * rmsnorm(swiglu(h, ffn2_w_gu[l], ffn2_w_down[l]), ffn2_post_g[l])
    return x
```

```python
import functools
import math

import jax
import jax.numpy as jnp
from jax import lax
from jax.experimental import pallas as pl
from jax.experimental.pallas import tpu as pltpu

D_MODEL = 1024
MLA_HEADS = 4
QK_NOPE_DIM = 128
QK_ROPE_DIM = 64
QK_DIM = QK_NOPE_DIM + QK_ROPE_DIM
V_HEAD_DIM = 128
Q_LORA_RANK = 384
KV_LORA_RANK = 256
MLA_WIDTH = MLA_HEADS * V_HEAD_DIM
ROPE_THETA = 10000.0
LRU_WIDTH = D_MODEL - MLA_WIDTH
LRU_BLOCKS = 8
CONV_WIDTH = 4
LRU_C = 8.0
XA_HEADS = 4
XA_HEAD_DIM = 128
XA_WIDTH = XA_HEADS * XA_HEAD_DIM
D_FF = 2816
EPS = 1e-6
NEG_INF = -1e30

LANES = 128
SUBLANES = 8
MXU_DIM = 256
VMEM_BYTES_V7X = 64 * 1024 * 1024

QK_PAD = 2 * LANES

ROW_TILE = 512
ATTN_TILE = 512
LRU_TILE = 512
FFN_CHUNKS = (768, 768, 768, 512)
assert sum(FFN_CHUNKS) == D_FF and all(c % MXU_DIM == 0 for c in FFN_CHUNKS)

_BF16 = jnp.bfloat16
_F32 = jnp.float32


def _rmsnorm(x, g):
    return x * lax.rsqrt(jnp.mean(x * x, axis=-1, keepdims=True) + EPS) * g


def _const_spec(shape):
    nd = len(shape)
    return pl.BlockSpec(shape, lambda *_: (0,) * nd, pipeline_mode=pl.Buffered(1))


def _params(semantics, vmem_mb):
    return pltpu.CompilerParams(dimension_semantics=semantics,
                                vmem_limit_bytes=vmem_mb * 1024 * 1024)


def _ffn_kernel(x_ref, gpre_ref, wgu_ref, wd_ref, gpost_ref, o_ref):
    x = x_ref[...]
    h = _rmsnorm(x, gpre_ref[...]).astype(_BF16)
    acc = None
    off = 0
    for c in FFN_CHUNKS:
        g = jnp.dot(h, wgu_ref[:, off:off + c], preferred_element_type=_F32)
        u = jnp.dot(h, wgu_ref[:, D_FF + off:D_FF + off + c], preferred_element_type=_F32)
        a = (jax.nn.silu(g) * u).astype(_BF16)
        part = jnp.dot(a, wd_ref[off:off + c, :], preferred_element_type=_F32)
        acc = part if acc is None else acc + part
        off += c
    o_ref[...] = x + 0.5 * _rmsnorm(acc, gpost_ref[...])


def _ffn(x, g_pre, w_gu, w_down, g_post):
    n = x.shape[0]
    row = pl.BlockSpec((ROW_TILE, D_MODEL), lambda i: (i, 0))
    return pl.pallas_call(
        _ffn_kernel,
        grid=(n // ROW_TILE,),
        in_specs=[row, _const_spec((1, D_MODEL)), _const_spec((D_MODEL, 2 * D_FF)),
                  _const_spec((D_FF, D_MODEL)), _const_spec((1, D_MODEL))],
        out_specs=row,
        out_shape=jax.ShapeDtypeStruct((n, D_MODEL), _F32),
        compiler_params=_params(("parallel",), 48),
        name="ffn",
    )(x, g_pre, w_gu, w_down, g_post)


_C_Q = 0
_C_KV = _C_Q + Q_LORA_RANK
_C_U = _C_KV + KV_LORA_RANK
_C_GATE = _C_U + LRU_WIDTH
_C_KPE = _C_GATE + LRU_WIDTH
_IN_COLS_EXT = _C_KPE + 2 * QK_ROPE_DIM


def _mixin_kernel(x_ref, pos_ref, inv_ref, gpre_ref, win_ref, gq_ref, wuq_ref, gkv_ref,
                  wk_ref, wvt_ref, q_ref, k_ref, vt_ref, u_ref, gate_ref):
    h = _rmsnorm(x_ref[...], gpre_ref[...]).astype(_BF16)
    z = jnp.dot(h, win_ref[...], preferred_element_type=_F32)
    u_ref[...] = z[:, _C_U:_C_GATE]
    gate_ref[...] = z[:, _C_GATE:_C_KPE]

    ang = pos_ref[...].astype(_F32) * inv_ref[...]
    lane = lax.broadcasted_iota(jnp.int32, ang.shape, 1)
    sin = jnp.sin(ang)
    table = jnp.where(lane < QK_ROPE_DIM, jnp.cos(ang),
                      jnp.where(lane < QK_ROPE_DIM + QK_ROPE_DIM // 2, -sin, sin))
    low = (lane < QK_ROPE_DIM).astype(_F32)

    def rope(pair):
        prod = pair * table
        return prod + pltpu.roll(prod, QK_ROPE_DIM, axis=1)

    k_rot = rope(z[:, _C_KPE:_IN_COLS_EXT]).astype(_BF16)

    c_q = _rmsnorm(z[:, _C_Q:_C_KV], gq_ref[...]).astype(_BF16)
    q_all = jnp.dot(c_q, wuq_ref[...], preferred_element_type=_F32)
    c_kv = _rmsnorm(z[:, _C_KV:_C_U], gkv_ref[...]).astype(_BF16)
    k_nope = jnp.dot(c_kv, wk_ref[...], preferred_element_type=_F32)
    v_t = lax.dot_general(wvt_ref[...], c_kv, (((1,), (1,)), ((), ())),
                          preferred_element_type=_F32)
    for hd in range(MLA_HEADS):
        qh = q_all[:, hd * QK_PAD:(hd + 1) * QK_PAD]
        q_ref[hd, :, 0:LANES] = qh[:, 0:LANES].astype(_BF16)
        q_ref[hd, :, LANES:QK_PAD] = (rope(qh[:, LANES:QK_PAD]) * low).astype(_BF16)
        k_ref[hd, :, 0:LANES] = k_nope[:, hd * LANES:(hd + 1) * LANES].astype(_BF16)
        k_ref[hd, :, LANES:QK_PAD] = k_rot
        vt_ref[hd, 0] = v_t[hd * V_HEAD_DIM:(hd + 1) * V_HEAD_DIM, :].astype(_BF16)


def _mixin(x, pos, inv, g_pre, w_in, g_q, w_uq, g_kv, w_k, w_vt, batch, seq):
    n = x.shape[0]
    nsb = seq // ROW_TILE
    assert ROW_TILE == ATTN_TILE
    row = lambda w: pl.BlockSpec((ROW_TILE, w), lambda i: (i, 0))
    head = lambda w: pl.BlockSpec((None, MLA_HEADS, ROW_TILE, w),
                                  lambda i: (i // nsb, 0, i % nsb, 0))
    return pl.pallas_call(
        _mixin_kernel,
        grid=(n // ROW_TILE,),
        in_specs=[row(D_MODEL), row(1), _const_spec((1, LANES)), _const_spec((1, D_MODEL)),
                  _const_spec((D_MODEL, _IN_COLS_EXT)), _const_spec((1, Q_LORA_RANK)),
                  _const_spec((Q_LORA_RANK, MLA_HEADS * QK_PAD)),
                  _const_spec((1, KV_LORA_RANK)),
                  _const_spec((KV_LORA_RANK, MLA_HEADS * QK_NOPE_DIM)),
                  _const_spec((MLA_HEADS * V_HEAD_DIM, KV_LORA_RANK))],
        out_specs=[head(QK_PAD), head(QK_PAD),
                   pl.BlockSpec((None, MLA_HEADS, 1, V_HEAD_DIM, ATTN_TILE),
                                lambda i: (i // nsb, 0, i % nsb, 0, 0)),
                   row(LRU_WIDTH), row(LRU_WIDTH)],
        out_shape=[jax.ShapeDtypeStruct((batch, MLA_HEADS, seq, QK_PAD), _BF16),
                   jax.ShapeDtypeStruct((batch, MLA_HEADS, seq, QK_PAD), _BF16),
                   jax.ShapeDtypeStruct((batch, MLA_HEADS, seq // ATTN_TILE, V_HEAD_DIM,
                                         ATTN_TILE), _BF16),
                   jax.ShapeDtypeStruct((n, LRU_WIDTH), _F32),
                   jax.ShapeDtypeStruct((n, LRU_WIDTH), _F32)],
        compiler_params=_params(("parallel",), 40),
        name="mixer_in",
    )(x, pos, inv, g_pre, w_in, g_q, w_uq, g_kv, w_k, w_vt)


_EXP2_SCALE = (1.0 / math.sqrt(QK_DIM)) * math.log2(math.e)


def _attn_kernel(q_ref, k_ref, vt_ref, o_ref, m_sc, l_sc, acc_sc):
    qi = pl.program_id(2)
    q = q_ref[...]
    m_sc[...] = jnp.full(m_sc.shape, NEG_INF, _F32)
    l_sc[...] = jnp.zeros(l_sc.shape, _F32)
    acc_sc[...] = jnp.zeros(acc_sc.shape, _F32)

    def step(j, masked):
        k = k_ref[pl.ds(pl.multiple_of(j * ATTN_TILE, ATTN_TILE), ATTN_TILE), :]
        s = lax.dot_general(k, q, (((1,), (1,)), ((), ())),
                            preferred_element_type=_F32)
        if masked:
            key = lax.broadcasted_iota(jnp.int32, s.shape, 0)
            qry = lax.broadcasted_iota(jnp.int32, s.shape, 1)
            s = jnp.where(key <= qry, s, NEG_INF)
        m_old = m_sc[...]
        m_new = jnp.maximum(m_old, jnp.max(s, axis=0, keepdims=True))
        p = jnp.exp2((s - m_new) * _EXP2_SCALE)
        alpha = jnp.exp2((m_old - m_new) * _EXP2_SCALE)
        l_sc[...] = alpha * l_sc[...] + jnp.sum(p, axis=0, keepdims=True)
        pv = jnp.dot(vt_ref[j], p.astype(_BF16), preferred_element_type=_F32)
        acc_sc[...] = alpha * acc_sc[...] + pv
        m_sc[...] = m_new

    def body(j, carry):
        step(j, False)
        return carry

    lax.fori_loop(0, qi, body, 0)
    step(qi, True)
    out = acc_sc[...] / l_sc[...]
    o_ref[...] = out.T.astype(o_ref.dtype)


def _attention(q, k, vt, batch, seq):
    nq = seq // ATTN_TILE
    return pl.pallas_call(
        _attn_kernel,
        grid=(batch, MLA_HEADS, nq),
        in_specs=[pl.BlockSpec((None, None, ATTN_TILE, QK_PAD), lambda b, h, i: (b, h, i, 0)),
                  pl.BlockSpec((None, None, seq, QK_PAD), lambda b, h, i: (b, h, 0, 0)),
                  pl.BlockSpec((None, None, nq, V_HEAD_DIM, ATTN_TILE),
                               lambda b, h, i: (b, h, 0, 0, 0))],
        out_specs=pl.BlockSpec((None, ATTN_TILE, V_HEAD_DIM), lambda b, h, i: (b, i, h)),
        out_shape=jax.ShapeDtypeStruct((batch, seq, MLA_WIDTH), _BF16),
        scratch_shapes=[pltpu.VMEM((1, ATTN_TILE), _F32), pltpu.VMEM((1, ATTN_TILE), _F32),
                        pltpu.VMEM((V_HEAD_DIM, ATTN_TILE), _F32)],
        compiler_params=_params(("parallel", "parallel", "arbitrary"), 40),
        name="mla_attention",
    )(q, k, vt)


def _shift_rows(x, d, fill):
    t, c = x.shape
    if d % SUBLANES == 0:
        return jnp.concatenate([jnp.full((d, c), fill, x.dtype), x[:t - d]], axis=0)
    row = lax.broadcasted_iota(jnp.int32, x.shape, 0)
    return jnp.where(row < d, fill, pltpu.roll(x, d, axis=0))


def _lru_kernel(u_ref, gate_ref, cw_ref, cb_ref, wax_ref, bax_ref, lam_ref, o_ref,
                tail_sc, h_sc):
    @pl.when(pl.program_id(1) == 0)
    def _():
        tail_sc[...] = jnp.zeros(tail_sc.shape, _F32)
        h_sc[...] = jnp.zeros(h_sc.shape, _F32)

    u = u_ref[...]
    tail = tail_sc[...]
    row8 = lax.broadcasted_iota(jnp.int32, tail.shape, 0)
    xc = cb_ref[...] + u * cw_ref[CONV_WIDTH - 1:CONV_WIDTH, :]
    for d in range(1, CONV_WIDTH):
        rolled = pltpu.roll(u, d, axis=0)
        head = jnp.where(row8 < d, pltpu.roll(tail, d, axis=0), rolled[:SUBLANES])
        shifted = jnp.concatenate([head, rolled[SUBLANES:]], axis=0)
        xc = xc + shifted * cw_ref[CONV_WIDTH - 1 - d:CONV_WIDTH - d, :]
    tail_sc[...] = u[u.shape[0] - SUBLANES:, :]

    pre = jnp.dot(xc.astype(_BF16), wax_ref[...], preferred_element_type=_F32) + bax_ref[...]
    r = jax.nn.sigmoid(pre[:, :LRU_WIDTH])
    i = jax.nn.sigmoid(pre[:, LRU_WIDTH:])
    neg_lam = -lam_ref[...]
    softplus = jnp.maximum(neg_lam, 0.0) + jnp.log1p(jnp.exp(-jnp.abs(neg_lam)))
    log_a = -LRU_C * r * softplus
    a = jnp.exp(log_a)
    b = jnp.sqrt(-jnp.tanh(log_a) * (a * a + 1.0)) * (i * xc)

    d = 1
    while d < a.shape[0]:
        b = a * _shift_rows(b, d, 0.0) + b
        a = a * _shift_rows(a, d, 1.0)
        d *= 2
    hseq = b + a * h_sc[...]
    h_sc[...] = hseq[hseq.shape[0] - 1:, :]
    o_ref[...] = (hseq * jax.nn.gelu(gate_ref[...])).astype(o_ref.dtype)


def _rglru(u, gate, conv_w, conv_b, w_ax, b_ax, lam, batch, seq):
    nst = seq // LRU_TILE
    row = pl.BlockSpec((LRU_TILE, LRU_WIDTH), lambda b, s: (b * nst + s, 0))
    return pl.pallas_call(
        _lru_kernel,
        grid=(batch, nst),
        in_specs=[row, row, _const_spec((CONV_WIDTH, LRU_WIDTH)), _const_spec((1, LRU_WIDTH)),
                  _const_spec((LRU_WIDTH, 2 * LRU_WIDTH)), _const_spec((1, 2 * LRU_WIDTH)),
                  _const_spec((1, LRU_WIDTH))],
        out_specs=row,
        out_shape=jax.ShapeDtypeStruct((batch * seq, LRU_WIDTH), _BF16),
        scratch_shapes=[pltpu.VMEM((SUBLANES, LRU_WIDTH), _F32), pltpu.VMEM((1, LRU_WIDTH), _F32)],
        compiler_params=_params(("parallel", "arbitrary"), 40),
        name="rglru",
    )(u, gate, conv_w, conv_b, w_ax, b_ax, lam)


def _memkv_kernel(mem_ref, g_ref, w_ref, o_ref):
    m = _rmsnorm(mem_ref[...], g_ref[...]).astype(_BF16)
    o_ref[...] = jnp.dot(m, w_ref[...], preferred_element_type=_F32).astype(o_ref.dtype)


def _memkv(mem, g, w_kv):
    batch, mlen, _ = mem.shape
    return pl.pallas_call(
        _memkv_kernel,
        grid=(batch,),
        in_specs=[pl.BlockSpec((None, mlen, D_MODEL), lambda b: (b, 0, 0)),
                  _const_spec((1, D_MODEL)), _const_spec((D_MODEL, 2 * XA_WIDTH))],
        out_specs=pl.BlockSpec((None, mlen, 2 * XA_WIDTH), lambda b: (b, 0, 0)),
        out_shape=jax.ShapeDtypeStruct((batch, mlen, 2 * XA_WIDTH), _BF16),
        compiler_params=_params(("parallel",), 32),
        name="mem_kv",
    )(mem, g, w_kv)


def _mixout_kernel(x_ref, ymla_ref, ylru_ref, wout_ref, gmix_ref, gxa_ref, wq_ref, kv_ref,
                   wo_ref, gxo_ref, o_ref):
    y = jnp.dot(ymla_ref[...], wout_ref[0:MLA_WIDTH, :], preferred_element_type=_F32)
    y = y + jnp.dot(ylru_ref[...], wout_ref[MLA_WIDTH:D_MODEL, :], preferred_element_type=_F32)
    x = x_ref[...] + _rmsnorm(y, gmix_ref[...])

    h = _rmsnorm(x, gxa_ref[...]).astype(_BF16)
    q = jnp.dot(h, wq_ref[...], preferred_element_type=_F32).astype(_BF16)
    scale = 1.0 / math.sqrt(XA_HEAD_DIM)
    heads = []
    for hd in range(XA_HEADS):
        lo = hd * XA_HEAD_DIM
        kh = kv_ref[:, lo:lo + XA_HEAD_DIM]
        vh = kv_ref[:, XA_WIDTH + lo:XA_WIDTH + lo + XA_HEAD_DIM]
        s = lax.dot_general(q[:, lo:lo + XA_HEAD_DIM], kh, (((1,), (1,)), ((), ())),
                            preferred_element_type=_F32) * scale
        e = jnp.exp(s - jnp.max(s, axis=-1, keepdims=True))
        p = (e / jnp.sum(e, axis=-1, keepdims=True)).astype(_BF16)
        heads.append(jnp.dot(p, vh, preferred_element_type=_F32).astype(_BF16))
    o = jnp.concatenate(heads, axis=-1)
    y2 = jnp.dot(o, wo_ref[...], preferred_element_type=_F32)
    o_ref[...] = x + _rmsnorm(y2, gxo_ref[...])


def _mixout(x, y_mla, y_lru, w_out, g_mix, g_xa, w_q, kv, w_o, g_xo, seq):
    n = x.shape[0]
    nsb = seq // ROW_TILE
    mlen = kv.shape[1]
    row = lambda w: pl.BlockSpec((ROW_TILE, w), lambda i: (i, 0))
    return pl.pallas_call(
        _mixout_kernel,
        grid=(n // ROW_TILE,),
        in_specs=[row(D_MODEL), row(MLA_WIDTH), row(LRU_WIDTH),
                  _const_spec((D_MODEL, D_MODEL)), _const_spec((1, D_MODEL)),
                  _const_spec((1, D_MODEL)), _const_spec((D_MODEL, XA_WIDTH)),
                  pl.BlockSpec((None, mlen, 2 * XA_WIDTH), lambda i: (i // nsb, 0, 0)),
                  _const_spec((XA_WIDTH, D_MODEL)), _const_spec((1, D_MODEL))],
        out_specs=row(D_MODEL),
        out_shape=jax.ShapeDtypeStruct((n, D_MODEL), _F32),
        compiler_params=_params(("parallel",), 40),
        name="mixer_out_xattn",
    )(x, y_mla, y_lru, w_out, g_mix, g_xa, w_q, kv, w_o, g_xo)


def _swap_halves(w):
    half = w.shape[-1] // 2
    return jnp.concatenate([w[..., half:], w[..., :half]], axis=-1)


def _prep_w_in(w_in):
    o1 = Q_LORA_RANK
    o2 = o1 + KV_LORA_RANK
    o3 = o2 + QK_ROPE_DIM
    o4 = o3 + LRU_WIDTH
    k_pe = w_in[:, o2:o3]
    return jnp.concatenate([w_in[:, :o1], w_in[:, o1:o2], w_in[:, o3:o4], w_in[:, o4:],
                            k_pe, _swap_halves(k_pe)], axis=-1).astype(_BF16)


def _prep_w_uq(w_uq):
    w = w_uq.reshape(Q_LORA_RANK, MLA_HEADS, QK_DIM)
    pe = w[..., QK_NOPE_DIM:]
    w = jnp.concatenate([w[..., :QK_NOPE_DIM], pe, _swap_halves(pe)], axis=-1)
    return w.reshape(Q_LORA_RANK, MLA_HEADS * QK_PAD).astype(_BF16)


def _prep_w_ukv(w_ukv):
    w = w_ukv.reshape(KV_LORA_RANK, MLA_HEADS, QK_NOPE_DIM + V_HEAD_DIM)
    w_k = w[..., :QK_NOPE_DIM].reshape(KV_LORA_RANK, MLA_HEADS * QK_NOPE_DIM)
    w_v = w[..., QK_NOPE_DIM:].reshape(KV_LORA_RANK, MLA_HEADS * V_HEAD_DIM)
    return w_k.astype(_BF16), w_v.T.astype(_BF16)


def _block_diag(w):
    nb, d, e = w.shape
    eye = jnp.eye(nb, dtype=w.dtype)
    return (eye[:, None, :, None] * w[:, :, None, :]).reshape(nb * d, nb * e)


def kernel(x, mem, positions, ffn1_pre_g, ffn1_w_gu, ffn1_w_down, ffn1_post_g, mix_pre_g, w_in, q_a_norm_g, w_uq, kv_a_norm_g, w_ukv, conv_w, conv_b, rg_w_a, rg_b_a, rg_w_x, rg_b_x, rg_lambda, w_out, mix_post_g, xa_pre_g, mem_norm_g, xa_w_q, xa_w_kv, xa_w_o, xa_post_g, ffn2_pre_g, ffn2_w_gu, ffn2_w_down, ffn2_post_g):
    batch, seq, _ = x.shape
    n = batch * seq
    depth = ffn1_pre_g.shape[0]
    bf = lambda w: w.astype(_BF16)
    vec = lambda g: g.reshape(1, -1)

    inv = ROPE_THETA ** (-jnp.arange(0, QK_ROPE_DIM, 2, dtype=_F32) / QK_ROPE_DIM)
    inv = jnp.tile(inv, 2 * LANES // QK_ROPE_DIM).reshape(1, LANES)
    pos = positions.reshape(n, 1)

    xf = x.reshape(n, D_MODEL)
    for l in range(depth):
        xf = _ffn(xf, vec(ffn1_pre_g[l]), bf(ffn1_w_gu[l]), bf(ffn1_w_down[l]),
                  vec(ffn1_post_g[l]))

        w_k, w_vt = _prep_w_ukv(w_ukv[l])
        q, k, vt, u, gate = _mixin(xf, pos, inv, vec(mix_pre_g[l]), _prep_w_in(w_in[l]),
                                   vec(q_a_norm_g[l]), _prep_w_uq(w_uq[l]),
                                   vec(kv_a_norm_g[l]), w_k, w_vt, batch, seq)
        y_mla = _attention(q, k, vt, batch, seq).reshape(n, MLA_WIDTH)
        w_ax = bf(jnp.concatenate([_block_diag(rg_w_a[l]), _block_diag(rg_w_x[l])], axis=-1))
        b_ax = jnp.concatenate([rg_b_a[l].reshape(1, -1), rg_b_x[l].reshape(1, -1)], axis=-1)
        y_lru = _rglru(u, gate, conv_w[l], vec(conv_b[l]), w_ax, b_ax, vec(rg_lambda[l]),
                       batch, seq)

        kv = _memkv(mem, vec(mem_norm_g[l]), bf(xa_w_kv[l]))
        xf = _mixout(xf, y_mla, y_lru, bf(w_out[l]), vec(mix_post_g[l]), vec(xa_pre_g[l]),
                     bf(xa_w_q[l]), kv, bf(xa_w_o[l]), vec(xa_post_g[l]), seq)

        xf = _ffn(xf, vec(ffn2_pre_g[l]), bf(ffn2_w_gu[l]), bf(ffn2_w_down[l]),
                  vec(ffn2_post_g[l]))
    return xf.reshape(batch, seq, D_MODEL)
```

```python
import functools
import math

import jax
import jax.numpy as jnp
from jax import lax
from jax.experimental import pallas as pl
from jax.experimental.pallas import tpu as pltpu

D_MODEL = 1024
MLA_HEADS = 4
QK_NOPE_DIM = 128
QK_ROPE_DIM = 64
QK_DIM = QK_NOPE_DIM + QK_ROPE_DIM
V_HEAD_DIM = 128
Q_LORA_RANK = 384
KV_LORA_RANK = 256
MLA_WIDTH = MLA_HEADS * V_HEAD_DIM
ROPE_THETA = 10000.0
LRU_WIDTH = D_MODEL - MLA_WIDTH
LRU_BLOCKS = 8
CONV_WIDTH = 4
LRU_C = 8.0
XA_HEADS = 4
XA_HEAD_DIM = 128
XA_WIDTH = XA_HEADS * XA_HEAD_DIM
D_FF = 2816
EPS = 1e-6
NEG_INF = -1e30

LANES = 128
SUBLANES = 8
MXU_DIM = 256
VMEM_BYTES_V7X = 64 * 1024 * 1024

QK_PAD = 2 * LANES

ROW_TILE = 512
ATTN_TILE = 512
LRU_TILE = 512
FFN_CHUNKS = (768, 768, 768, 512)
assert sum(FFN_CHUNKS) == D_FF and all(c % MXU_DIM == 0 for c in FFN_CHUNKS)

_BF16 = jnp.bfloat16
_F32 = jnp.float32


def _rmsnorm(x, g):
    return x * lax.rsqrt(jnp.mean(x * x, axis=-1, keepdims=True) + EPS) * g


def _const_spec(shape):
    nd = len(shape)
    return pl.BlockSpec(shape, lambda *_: (0,) * nd, pipeline_mode=pl.Buffered(1))


def _params(semantics, vmem_mb):
    return pltpu.CompilerParams(dimension_semantics=semantics,
                                vmem_limit_bytes=vmem_mb * 1024 * 1024)


def _ffn_kernel(x_ref, gpre_ref, wgu_ref, wd_ref, gpost_ref, o_ref):
    x = x_ref[...]
    h = _rmsnorm(x, gpre_ref[...]).astype(_BF16)
    acc = None
    off = 0
    for c in FFN_CHUNKS:
        g = jnp.dot(h, wgu_ref[:, off:off + c], preferred_element_type=_F32)
        u = jnp.dot(h, wgu_ref[:, D_FF + off:D_FF + off + c], preferred_element_type=_F32)
        a = (jax.nn.silu(g) * u).astype(_BF16)
        part = jnp.dot(a, wd_ref[off:off + c, :], preferred_element_type=_F32)
        acc = part if acc is None else acc + part
        off += c
    o_ref[...] = x + 0.5 * _rmsnorm(acc, gpost_ref[...])


def _ffn(x, g_pre, w_gu, w_down, g_post):
    n = x.shape[0]
    row = pl.BlockSpec((ROW_TILE, D_MODEL), lambda i: (i, 0))
    return pl.pallas_call(
        _ffn_kernel,
        grid=(n // ROW_TILE,),
        in_specs=[row, _const_spec((1, D_MODEL)), _const_spec((D_MODEL, 2 * D_FF)),
                  _const_spec((D_FF, D_MODEL)), _const_spec((1, D_MODEL))],
        out_specs=row,
        out_shape=jax.ShapeDtypeStruct((n, D_MODEL), _F32),
        compiler_params=_params(("parallel",), 48),
        name="ffn",
    )(x, g_pre, w_gu, w_down, g_post)


_EXP2_SCALE = (1.0 / math.sqrt(QK_DIM)) * math.log2(math.e)

_C_Q = 0
_C_KV = _C_Q + Q_LORA_RANK
_C_U = _C_KV + KV_LORA_RANK
_C_GATE = _C_U + LRU_WIDTH
_C_KPE = _C_GATE + LRU_WIDTH
_IN_COLS_EXT = _C_KPE + 2 * QK_ROPE_DIM


def _mixin_kernel(x_ref, pos_ref, inv_ref, gpre_ref, win_ref, gq_ref, wuq_ref, gkv_ref,
                  wk_ref, wvt_ref, q_ref, k_ref, vt_ref, u_ref, gate_ref, kmax_ref, *,
                  tiles_per_batch):
    @pl.when(pl.program_id(0) % tiles_per_batch == 0)
    def _():
        kmax_ref[...] = jnp.zeros(kmax_ref.shape, _F32)

    h = _rmsnorm(x_ref[...], gpre_ref[...]).astype(_BF16)
    z = jnp.dot(h, win_ref[...], preferred_element_type=_F32)
    u_ref[...] = z[:, _C_U:_C_GATE]
    gate_ref[...] = z[:, _C_GATE:_C_KPE]

    ang = pos_ref[...].astype(_F32) * inv_ref[...]
    lane = lax.broadcasted_iota(jnp.int32, ang.shape, 1)
    sin = jnp.sin(ang)
    table = jnp.where(lane < QK_ROPE_DIM, jnp.cos(ang),
                      jnp.where(lane < QK_ROPE_DIM + QK_ROPE_DIM // 2, -sin, sin))
    low = (lane < QK_ROPE_DIM).astype(_F32)

    def rope(pair):
        prod = pair * table
        return prod + pltpu.roll(prod, QK_ROPE_DIM, axis=1)

    k_rot = rope(z[:, _C_KPE:_IN_COLS_EXT])
    k_rot_sq = jnp.sum(k_rot * k_rot, axis=1, keepdims=True)
    k_rot = k_rot.astype(_BF16)

    c_q = _rmsnorm(z[:, _C_Q:_C_KV], gq_ref[...]).astype(_BF16)
    q_all = jnp.dot(c_q, wuq_ref[...], preferred_element_type=_F32) * _EXP2_SCALE
    c_kv = _rmsnorm(z[:, _C_KV:_C_U], gkv_ref[...]).astype(_BF16)
    k_nope = jnp.dot(c_kv, wk_ref[...], preferred_element_type=_F32)
    v_t = lax.dot_general(wvt_ref[...], c_kv, (((1,), (1,)), ((), ())),
                          preferred_element_type=_F32)
    for hd in range(MLA_HEADS):
        qh = q_all[:, hd * QK_PAD:(hd + 1) * QK_PAD]
        q_ref[hd, :, 0:LANES] = qh[:, 0:LANES].astype(_BF16)
        q_ref[hd, :, LANES:QK_PAD] = (rope(qh[:, LANES:QK_PAD]) * low).astype(_BF16)
        kh = k_nope[:, hd * LANES:(hd + 1) * LANES]
        k_ref[hd, :, 0:LANES] = kh.astype(_BF16)
        k_ref[hd, :, LANES:QK_PAD] = k_rot
        vt_ref[hd, 0] = v_t[hd * V_HEAD_DIM:(hd + 1) * V_HEAD_DIM, :].astype(_BF16)
        k_sq = jnp.max(jnp.sum(kh * kh, axis=1, keepdims=True) + k_rot_sq, axis=0, keepdims=True)
        kmax_ref[hd:hd + 1, :] = jnp.maximum(kmax_ref[hd:hd + 1, :], k_sq)


def _mixin(x, pos, inv, g_pre, w_in, g_q, w_uq, g_kv, w_k, w_vt, batch, seq):
    n = x.shape[0]
    nsb = seq // ROW_TILE
    assert ROW_TILE == ATTN_TILE
    row = lambda w: pl.BlockSpec((ROW_TILE, w), lambda i: (i, 0))
    head = lambda w: pl.BlockSpec((None, MLA_HEADS, ROW_TILE, w),
                                  lambda i: (i // nsb, 0, i % nsb, 0))
    return pl.pallas_call(
        functools.partial(_mixin_kernel, tiles_per_batch=nsb),
        grid=(n // ROW_TILE,),
        in_specs=[row(D_MODEL), row(1), _const_spec((1, LANES)), _const_spec((1, D_MODEL)),
                  _const_spec((D_MODEL, _IN_COLS_EXT)), _const_spec((1, Q_LORA_RANK)),
                  _const_spec((Q_LORA_RANK, MLA_HEADS * QK_PAD)),
                  _const_spec((1, KV_LORA_RANK)),
                  _const_spec((KV_LORA_RANK, MLA_HEADS * QK_NOPE_DIM)),
                  _const_spec((MLA_HEADS * V_HEAD_DIM, KV_LORA_RANK))],
        out_specs=[head(QK_PAD), head(QK_PAD),
                   pl.BlockSpec((None, MLA_HEADS, 1, V_HEAD_DIM, ATTN_TILE),
                                lambda i: (i // nsb, 0, i % nsb, 0, 0)),
                   row(LRU_WIDTH), row(LRU_WIDTH),
                   pl.BlockSpec((None, SUBLANES, LANES), lambda i: (i // nsb, 0, 0))],
        out_shape=[jax.ShapeDtypeStruct((batch, MLA_HEADS, seq, QK_PAD), _BF16),
                   jax.ShapeDtypeStruct((batch, MLA_HEADS, seq, QK_PAD), _BF16),
                   jax.ShapeDtypeStruct((batch, MLA_HEADS, seq // ATTN_TILE, V_HEAD_DIM,
                                         ATTN_TILE), _BF16),
                   jax.ShapeDtypeStruct((n, LRU_WIDTH), _F32),
                   jax.ShapeDtypeStruct((n, LRU_WIDTH), _F32),
                   jax.ShapeDtypeStruct((batch, SUBLANES, LANES), _F32)],
        compiler_params=_params(("arbitrary",), 40),
        name="mixer_in",
    )(x, pos, inv, g_pre, w_in, g_q, w_uq, g_kv, w_k, w_vt)


SHIFT_HEADROOM = 90.0
FIXED_SHIFT_UNROLL = 2
NORM_MARGIN = 1.02


def _attn_kernel(q_ref, k_ref, vt_ref, kmax_ref, o_ref, m_sc, l_sc, acc_sc):
    qi = pl.program_id(1)

    def scores(j, hd):
        k = k_ref[hd, pl.ds(pl.multiple_of(j * ATTN_TILE, ATTN_TILE), ATTN_TILE), :]
        return lax.dot_general(k, q_ref[hd], (((1,), (1,)), ((), ())),
                               preferred_element_type=_F32)

    def run_blocks(blocks, softmax_update):
        items = [(j, hd) for j in blocks for hd in range(MLA_HEADS)]
        s_next = scores(*items[0])
        pending = None
        for n, (j, hd) in enumerate(items):
            s = s_next
            if n + 1 < len(items):
                s_next = scores(*items[n + 1])
            if pending is not None:
                pending()
            pending = softmax_update(j, hd, s)
        pending()

    def diagonal(j, hd, s):
        key = lax.broadcasted_iota(jnp.int32, s.shape, 0)
        qry = lax.broadcasted_iota(jnp.int32, s.shape, 1)
        s = jnp.where(key <= qry, s, NEG_INF)
        m = jnp.max(s, axis=0, keepdims=True)
        p = jnp.exp2(s - m)
        m_sc[hd] = m
        l_sc[hd] = jnp.sum(p, axis=0, keepdims=True)
        p = p.astype(_BF16)

        def values():
            acc_sc[hd] = jnp.dot(vt_ref[hd, j], p, preferred_element_type=_F32)
        return values

    def fixed_shift(j, hd, s):
        p = jnp.exp2(s - m_sc[hd])
        l_sc[hd] = l_sc[hd] + jnp.sum(p, axis=0, keepdims=True)
        p = p.astype(_BF16)

        def values():
            acc_sc[hd] = acc_sc[hd] + jnp.dot(vt_ref[hd, j], p, preferred_element_type=_F32)
        return values

    def running_max(j, hd, s):
        m_old = m_sc[hd]
        m_new = jnp.maximum(m_old, jnp.max(s, axis=0, keepdims=True))
        p = jnp.exp2(s - m_new)
        alpha = jnp.exp2(m_old - m_new)
        l_sc[hd] = alpha * l_sc[hd] + jnp.sum(p, axis=0, keepdims=True)
        m_sc[hd] = m_new
        p = p.astype(_BF16)

        def values():
            pv = jnp.dot(vt_ref[hd, j], p, preferred_element_type=_F32)
            acc_sc[hd] = alpha * acc_sc[hd] + pv
        return values

    run_blocks([qi], diagonal)

    excess = jnp.zeros((1, ATTN_TILE), _F32)
    ones = jnp.ones((2 * SUBLANES, QK_PAD), _BF16)
    for hd in range(MLA_HEADS):
        q = q_ref[hd]
        q_sq = lax.dot_general(ones, q * q, (((1,), (1,)), ((), ())),
                               preferred_element_type=_F32)[0:1]
        k_sq = jnp.max(kmax_ref[hd:hd + 1, :], axis=1, keepdims=True)
        bound = jnp.sqrt(q_sq * k_sq) * NORM_MARGIN
        excess = jnp.maximum(excess, bound - m_sc[hd])
    fixed_ok = jnp.max(excess) <= SHIFT_HEADROOM

    def loop_with(update, unroll):
        def body(t, carry):
            run_blocks([t * unroll + r for r in range(unroll)], update)
            return carry
        lax.fori_loop(0, qi // unroll, body, 0)
        for r in range(unroll - 1, 0, -1):
            @pl.when(qi % unroll >= r)
            def _():
                run_blocks([qi - r], update)

    @pl.when(fixed_ok)
    def _():
        loop_with(fixed_shift, FIXED_SHIFT_UNROLL)

    @pl.when(jnp.logical_not(fixed_ok))
    def _():
        loop_with(running_max, 1)

    for hd in range(MLA_HEADS):
        out = acc_sc[hd] / l_sc[hd]
        o_ref[:, hd * V_HEAD_DIM:(hd + 1) * V_HEAD_DIM] = out.T.astype(o_ref.dtype)


def _attention(q, k, vt, kmax, batch, seq):
    nq = seq // ATTN_TILE
    return pl.pallas_call(
        _attn_kernel,
        grid=(batch, nq),
        in_specs=[pl.BlockSpec((None, MLA_HEADS, ATTN_TILE, QK_PAD), lambda b, i: (b, 0, i, 0)),
                  pl.BlockSpec((None, MLA_HEADS, seq, QK_PAD), lambda b, i: (b, 0, 0, 0),
                               pipeline_mode=pl.Buffered(1)),
                  pl.BlockSpec((None, MLA_HEADS, nq, V_HEAD_DIM, ATTN_TILE),
                               lambda b, i: (b, 0, 0, 0, 0), pipeline_mode=pl.Buffered(1)),
                  pl.BlockSpec((None, SUBLANES, LANES), lambda b, i: (b, 0, 0))],
        out_specs=pl.BlockSpec((None, ATTN_TILE, MLA_WIDTH), lambda b, i: (b, i, 0)),
        out_shape=jax.ShapeDtypeStruct((batch, seq, MLA_WIDTH), _BF16),
        scratch_shapes=[pltpu.VMEM((MLA_HEADS, 1, ATTN_TILE), _F32),
                        pltpu.VMEM((MLA_HEADS, 1, ATTN_TILE), _F32),
                        pltpu.VMEM((MLA_HEADS, V_HEAD_DIM, ATTN_TILE), _F32)],
        compiler_params=_params(("parallel", "arbitrary"), 48),
        name="mla_attention",
    )(q, k, vt, kmax)


def _shift_rows(x, d, fill):
    t, c = x.shape
    if d % SUBLANES == 0:
        return jnp.concatenate([jnp.full((d, c), fill, x.dtype), x[:t - d]], axis=0)
    row = lax.broadcasted_iota(jnp.int32, x.shape, 0)
    return jnp.where(row < d, fill, pltpu.roll(x, d, axis=0))


def _lru_kernel(u_ref, gate_ref, cw_ref, cb_ref, wax_ref, bax_ref, lam_ref, o_ref,
                tail_sc, h_sc):
    @pl.when(pl.program_id(1) == 0)
    def _():
        tail_sc[...] = jnp.zeros(tail_sc.shape, _F32)
        h_sc[...] = jnp.zeros(h_sc.shape, _F32)

    u = u_ref[...]
    tail = tail_sc[...]
    row8 = lax.broadcasted_iota(jnp.int32, tail.shape, 0)
    xc = cb_ref[...] + u * cw_ref[CONV_WIDTH - 1:CONV_WIDTH, :]
    for d in range(1, CONV_WIDTH):
        rolled = pltpu.roll(u, d, axis=0)
        head = jnp.where(row8 < d, pltpu.roll(tail, d, axis=0), rolled[:SUBLANES])
        shifted = jnp.concatenate([head, rolled[SUBLANES:]], axis=0)
        xc = xc + shifted * cw_ref[CONV_WIDTH - 1 - d:CONV_WIDTH - d, :]
    tail_sc[...] = u[u.shape[0] - SUBLANES:, :]

    pre = jnp.dot(xc.astype(_BF16), wax_ref[...], preferred_element_type=_F32) + bax_ref[...]
    r = jax.nn.sigmoid(pre[:, :LRU_WIDTH])
    i = jax.nn.sigmoid(pre[:, LRU_WIDTH:])
    neg_lam = -lam_ref[...]
    softplus = jnp.maximum(neg_lam, 0.0) + jnp.log1p(jnp.exp(-jnp.abs(neg_lam)))
    log_a = -LRU_C * r * softplus
    a = jnp.exp(log_a)
    b = jnp.sqrt(-jnp.tanh(log_a) * (a * a + 1.0)) * (i * xc)

    d = 1
    while d < a.shape[0]:
        b = a * _shift_rows(b, d, 0.0) + b
        a = a * _shift_rows(a, d, 1.0)
        d *= 2
    hseq = b + a * h_sc[...]
    h_sc[...] = hseq[hseq.shape[0] - 1:, :]
    o_ref[...] = (hseq * jax.nn.gelu(gate_ref[...])).astype(o_ref.dtype)


def _rglru(u, gate, conv_w, conv_b, w_ax, b_ax, lam, batch, seq):
    nst = seq // LRU_TILE
    row = pl.BlockSpec((LRU_TILE, LRU_WIDTH), lambda b, s: (b * nst + s, 0))
    return pl.pallas_call(
        _lru_kernel,
        grid=(batch, nst),
        in_specs=[row, row, _const_spec((CONV_WIDTH, LRU_WIDTH)), _const_spec((1, LRU_WIDTH)),
                  _const_spec((LRU_WIDTH, 2 * LRU_WIDTH)), _const_spec((1, 2 * LRU_WIDTH)),
                  _const_spec((1, LRU_WIDTH))],
        out_specs=row,
        out_shape=jax.ShapeDtypeStruct((batch * seq, LRU_WIDTH), _BF16),
        scratch_shapes=[pltpu.VMEM((SUBLANES, LRU_WIDTH), _F32), pltpu.VMEM((1, LRU_WIDTH), _F32)],
        compiler_params=_params(("parallel", "arbitrary"), 40),
        name="rglru",
    )(u, gate, conv_w, conv_b, w_ax, b_ax, lam)


def _memkv_kernel(mem_ref, g_ref, w_ref, o_ref):
    m = _rmsnorm(mem_ref[...], g_ref[...]).astype(_BF16)
    o_ref[...] = jnp.dot(m, w_ref[...], preferred_element_type=_F32).astype(o_ref.dtype)


def _memkv(mem, g, w_kv):
    batch, mlen, _ = mem.shape
    return pl.pallas_call(
        _memkv_kernel,
        grid=(batch,),
        in_specs=[pl.BlockSpec((None, mlen, D_MODEL), lambda b: (b, 0, 0)),
                  _const_spec((1, D_MODEL)), _const_spec((D_MODEL, 2 * XA_WIDTH))],
        out_specs=pl.BlockSpec((None, mlen, 2 * XA_WIDTH), lambda b: (b, 0, 0)),
        out_shape=jax.ShapeDtypeStruct((batch, mlen, 2 * XA_WIDTH), _BF16),
        compiler_params=_params(("parallel",), 32),
        name="mem_kv",
    )(mem, g, w_kv)


def _mixout_kernel(x_ref, ymla_ref, ylru_ref, wout_ref, gmix_ref, gxa_ref, wq_ref, kv_ref,
                   wo_ref, gxo_ref, o_ref):
    y = jnp.dot(ymla_ref[...], wout_ref[0:MLA_WIDTH, :], preferred_element_type=_F32)
    y = y + jnp.dot(ylru_ref[...], wout_ref[MLA_WIDTH:D_MODEL, :], preferred_element_type=_F32)
    x = x_ref[...] + _rmsnorm(y, gmix_ref[...])

    h = _rmsnorm(x, gxa_ref[...]).astype(_BF16)
    q = jnp.dot(h, wq_ref[...], preferred_element_type=_F32).astype(_BF16)
    scale = 1.0 / math.sqrt(XA_HEAD_DIM)
    heads = []
    for hd in range(XA_HEADS):
        lo = hd * XA_HEAD_DIM
        kh = kv_ref[:, lo:lo + XA_HEAD_DIM]
        vh = kv_ref[:, XA_WIDTH + lo:XA_WIDTH + lo + XA_HEAD_DIM]
        s = lax.dot_general(q[:, lo:lo + XA_HEAD_DIM], kh, (((1,), (1,)), ((), ())),
                            preferred_element_type=_F32) * scale
        e = jnp.exp(s - jnp.max(s, axis=-1, keepdims=True))
        p = (e / jnp.sum(e, axis=-1, keepdims=True)).astype(_BF16)
        heads.append(jnp.dot(p, vh, preferred_element_type=_F32).astype(_BF16))
    o = jnp.concatenate(heads, axis=-1)
    y2 = jnp.dot(o, wo_ref[...], preferred_element_type=_F32)
    o_ref[...] = x + _rmsnorm(y2, gxo_ref[...])


def _mixout(x, y_mla, y_lru, w_out, g_mix, g_xa, w_q, kv, w_o, g_xo, seq):
    n = x.shape[0]
    nsb = seq // ROW_TILE
    mlen = kv.shape[1]
    row = lambda w: pl.BlockSpec((ROW_TILE, w), lambda i: (i, 0))
    return pl.pallas_call(
        _mixout_kernel,
        grid=(n // ROW_TILE,),
        in_specs=[row(D_MODEL), row(MLA_WIDTH), row(LRU_WIDTH),
                  _const_spec((D_MODEL, D_MODEL)), _const_spec((1, D_MODEL)),
                  _const_spec((1, D_MODEL)), _const_spec((D_MODEL, XA_WIDTH)),
                  pl.BlockSpec((None, mlen, 2 * XA_WIDTH), lambda i: (i // nsb, 0, 0)),
                  _const_spec((XA_WIDTH, D_MODEL)), _const_spec((1, D_MODEL))],
        out_specs=row(D_MODEL),
        out_shape=jax.ShapeDtypeStruct((n, D_MODEL), _F32),
        compiler_params=_params(("parallel",), 40),
        name="mixer_out_xattn",
    )(x, y_mla, y_lru, w_out, g_mix, g_xa, w_q, kv, w_o, g_xo)


def _swap_halves(w):
    half = w.shape[-1] // 2
    return jnp.concatenate([w[..., half:], w[..., :half]], axis=-1)


def _prep_w_in(w_in):
    o1 = Q_LORA_RANK
    o2 = o1 + KV_LORA_RANK
    o3 = o2 + QK_ROPE_DIM
    o4 = o3 + LRU_WIDTH
    k_pe = w_in[:, o2:o3]
    return jnp.concatenate([w_in[:, :o1], w_in[:, o1:o2], w_in[:, o3:o4], w_in[:, o4:],
                            k_pe, _swap_halves(k_pe)], axis=-1).astype(_BF16)


def _prep_w_uq(w_uq):
    w = w_uq.reshape(Q_LORA_RANK, MLA_HEADS, QK_DIM)
    pe = w[..., QK_NOPE_DIM:]
    w = jnp.concatenate([w[..., :QK_NOPE_DIM], pe, _swap_halves(pe)], axis=-1)
    return w.reshape(Q_LORA_RANK, MLA_HEADS * QK_PAD).astype(_BF16)


def _prep_w_ukv(w_ukv):
    w = w_ukv.reshape(KV_LORA_RANK, MLA_HEADS, QK_NOPE_DIM + V_HEAD_DIM)
    w_k = w[..., :QK_NOPE_DIM].reshape(KV_LORA_RANK, MLA_HEADS * QK_NOPE_DIM)
    w_v = w[..., QK_NOPE_DIM:].reshape(KV_LORA_RANK, MLA_HEADS * V_HEAD_DIM)
    return w_k.astype(_BF16), w_v.T.astype(_BF16)


def _block_diag(w):
    nb, d, e = w.shape
    eye = jnp.eye(nb, dtype=w.dtype)
    return (eye[:, None, :, None] * w[:, :, None, :]).reshape(nb * d, nb * e)


def kernel(x, mem, positions, ffn1_pre_g, ffn1_w_gu, ffn1_w_down, ffn1_post_g, mix_pre_g, w_in, q_a_norm_g, w_uq, kv_a_norm_g, w_ukv, conv_w, conv_b, rg_w_a, rg_b_a, rg_w_x, rg_b_x, rg_lambda, w_out, mix_post_g, xa_pre_g, mem_norm_g, xa_w_q, xa_w_kv, xa_w_o, xa_post_g, ffn2_pre_g, ffn2_w_gu, ffn2_w_down, ffn2_post_g):
    batch, seq, _ = x.shape
    n = batch * seq
    depth = ffn1_pre_g.shape[0]
    bf = lambda w: w.astype(_BF16)
    vec = lambda g: g.reshape(1, -1)

    inv = ROPE_THETA ** (-jnp.arange(0, QK_ROPE_DIM, 2, dtype=_F32) / QK_ROPE_DIM)
    inv = jnp.tile(inv, 2 * LANES // QK_ROPE_DIM).reshape(1, LANES)
    pos = positions.reshape(n, 1)

    xf = x.reshape(n, D_MODEL)
    for l in range(depth):
        xf = _ffn(xf, vec(ffn1_pre_g[l]), bf(ffn1_w_gu[l]), bf(ffn1_w_down[l]),
                  vec(ffn1_post_g[l]))

        w_k, w_vt = _prep_w_ukv(w_ukv[l])
        q, k, vt, u, gate, kmax = _mixin(xf, pos, inv, vec(mix_pre_g[l]), _prep_w_in(w_in[l]),
                                         vec(q_a_norm_g[l]), _prep_w_uq(w_uq[l]),
                                         vec(kv_a_norm_g[l]), w_k, w_vt, batch, seq)
        y_mla = _attention(q, k, vt, kmax, batch, seq).reshape(n, MLA_WIDTH)
        w_ax = bf(jnp.concatenate([_block_diag(rg_w_a[l]), _block_diag(rg_w_x[l])], axis=-1))
        b_ax = jnp.concatenate([rg_b_a[l].reshape(1, -1), rg_b_x[l].reshape(1, -1)], axis=-1)
        y_lru = _rglru(u, gate, conv_w[l], vec(conv_b[l]), w_ax, b_ax, vec(rg_lambda[l]),
                       batch, seq)

        kv = _memkv(mem, vec(mem_norm_g[l]), bf(xa_w_kv[l]))
        xf = _mixout(xf, y_mla, y_lru, bf(w_out[l]), vec(mix_post_g[l]), vec(xa_pre_g[l]),
                     bf(xa_w_q[l]), kv, bf(xa_w_o[l]), vec(xa_post_g[l]), seq)

        xf = _ffn(xf, vec(ffn2_pre_g[l]), bf(ffn2_w_gu[l]), bf(ffn2_w_down[l]),
                  vec(ffn2_post_g[l]))
    return xf.reshape(batch, seq, D_MODEL)
```

```python
import functools
import math

import jax
import jax.numpy as jnp
from jax import lax
from jax.experimental import pallas as pl
from jax.experimental.pallas import tpu as pltpu

D_MODEL = 1024
MLA_HEADS = 4
QK_NOPE_DIM = 128
QK_ROPE_DIM = 64
QK_DIM = QK_NOPE_DIM + QK_ROPE_DIM
V_HEAD_DIM = 128
Q_LORA_RANK = 384
KV_LORA_RANK = 256
MLA_WIDTH = MLA_HEADS * V_HEAD_DIM
ROPE_THETA = 10000.0
LRU_WIDTH = D_MODEL - MLA_WIDTH
LRU_BLOCKS = 8
CONV_WIDTH = 4
LRU_C = 8.0
XA_HEADS = 4
XA_HEAD_DIM = 128
XA_WIDTH = XA_HEADS * XA_HEAD_DIM
D_FF = 2816
EPS = 1e-6
NEG_INF = -1e30

LANES = 128
SUBLANES = 8
MXU_DIM = 256
VMEM_BYTES_V7X = 64 * 1024 * 1024

QK_PAD = 2 * LANES

ROW_TILE = 512
ATTN_TILE = 512
LRU_TILE = 512
FFN_CHUNKS = (768, 768, 768, 512)
assert sum(FFN_CHUNKS) == D_FF and all(c % MXU_DIM == 0 for c in FFN_CHUNKS)

_BF16 = jnp.bfloat16
_F32 = jnp.float32


def _rmsnorm(x, g):
    return x * lax.rsqrt(jnp.mean(x * x, axis=-1, keepdims=True) + EPS) * g


def _const_spec(shape):
    nd = len(shape)
    return pl.BlockSpec(shape, lambda *_: (0,) * nd, pipeline_mode=pl.Buffered(1))


def _params(semantics, vmem_mb):
    return pltpu.CompilerParams(dimension_semantics=semantics,
                                vmem_limit_bytes=vmem_mb * 1024 * 1024)


def _ffn_body(x, gpre_ref, wgu_ref, wd_ref, gpost_ref):
    h = _rmsnorm(x, gpre_ref[...]).astype(_BF16)
    acc = None
    off = 0
    for c in FFN_CHUNKS:
        g = jnp.dot(h, wgu_ref[:, off:off + c], preferred_element_type=_F32)
        u = jnp.dot(h, wgu_ref[:, D_FF + off:D_FF + off + c], preferred_element_type=_F32)
        a = (jax.nn.silu(g) * u).astype(_BF16)
        part = jnp.dot(a, wd_ref[off:off + c, :], preferred_element_type=_F32)
        acc = part if acc is None else acc + part
        off += c
    return x + 0.5 * _rmsnorm(acc, gpost_ref[...])


def _ffn_specs():
    return [_const_spec((1, D_MODEL)), _const_spec((D_MODEL, 2 * D_FF)),
            _const_spec((D_FF, D_MODEL)), _const_spec((1, D_MODEL))]


_EXP2_SCALE = (1.0 / math.sqrt(QK_DIM)) * math.log2(math.e)

_C_Q = 0
_C_KV = _C_Q + Q_LORA_RANK
_C_U = _C_KV + KV_LORA_RANK
_C_GATE = _C_U + LRU_WIDTH
_C_KPE = _C_GATE + LRU_WIDTH
_IN_COLS_EXT = _C_KPE + 2 * QK_ROPE_DIM


def _ffn_mixin_kernel(x_ref, fgpre_ref, fwgu_ref, fwd_ref, fgpost_ref,
                      pos_ref, inv_ref, gpre_ref, win_ref, gq_ref, wuq_ref, gkv_ref,
                      wk_ref, wvt_ref, x1_ref, q_ref, k_ref, vt_ref, u_ref, gate_ref, kmax_ref, *,
                      tiles_per_batch):
    @pl.when(pl.program_id(0) % tiles_per_batch == 0)
    def _():
        kmax_ref[...] = jnp.zeros(kmax_ref.shape, _F32)

    x1 = _ffn_body(x_ref[...], fgpre_ref, fwgu_ref, fwd_ref, fgpost_ref)
    x1_ref[...] = x1
    h = _rmsnorm(x1, gpre_ref[...]).astype(_BF16)
    z = jnp.dot(h, win_ref[...], preferred_element_type=_F32)
    u_ref[...] = z[:, _C_U:_C_GATE]
    gate_ref[...] = z[:, _C_GATE:_C_KPE]

    ang = pos_ref[...].astype(_F32) * inv_ref[...]
    lane = lax.broadcasted_iota(jnp.int32, ang.shape, 1)
    sin = jnp.sin(ang)
    table = jnp.where(lane < QK_ROPE_DIM, jnp.cos(ang),
                      jnp.where(lane < QK_ROPE_DIM + QK_ROPE_DIM // 2, -sin, sin))
    low = (lane < QK_ROPE_DIM).astype(_F32)

    def rope(pair):
        prod = pair * table
        return prod + pltpu.roll(prod, QK_ROPE_DIM, axis=1)

    k_rot = rope(z[:, _C_KPE:_IN_COLS_EXT])
    k_rot_sq = jnp.sum(k_rot * k_rot, axis=1, keepdims=True)
    k_rot = k_rot.astype(_BF16)

    c_q = _rmsnorm(z[:, _C_Q:_C_KV], gq_ref[...]).astype(_BF16)
    q_all = jnp.dot(c_q, wuq_ref[...], preferred_element_type=_F32) * _EXP2_SCALE
    c_kv = _rmsnorm(z[:, _C_KV:_C_U], gkv_ref[...]).astype(_BF16)
    k_nope = jnp.dot(c_kv, wk_ref[...], preferred_element_type=_F32)
    v_t = lax.dot_general(wvt_ref[...], c_kv, (((1,), (1,)), ((), ())),
                          preferred_element_type=_F32)
    for hd in range(MLA_HEADS):
        qh = q_all[:, hd * QK_PAD:(hd + 1) * QK_PAD]
        q_ref[hd, :, 0:LANES] = qh[:, 0:LANES].astype(_BF16)
        q_ref[hd, :, LANES:QK_PAD] = (rope(qh[:, LANES:QK_PAD]) * low).astype(_BF16)
        kh = k_nope[:, hd * LANES:(hd + 1) * LANES]
        k_ref[hd, :, 0:LANES] = kh.astype(_BF16)
        k_ref[hd, :, LANES:QK_PAD] = k_rot
        vt_ref[hd, 0] = v_t[hd * V_HEAD_DIM:(hd + 1) * V_HEAD_DIM, :].astype(_BF16)
        k_sq = jnp.max(jnp.sum(kh * kh, axis=1, keepdims=True) + k_rot_sq, axis=0, keepdims=True)
        kmax_ref[hd:hd + 1, :] = jnp.maximum(kmax_ref[hd:hd + 1, :], k_sq)


def _ffn_mixin(x, ffn_args, pos, inv, g_pre, w_in, g_q, w_uq, g_kv, w_k, w_vt, batch, seq):
    n = x.shape[0]
    nsb = seq // ROW_TILE
    assert ROW_TILE == ATTN_TILE
    row = lambda w: pl.BlockSpec((ROW_TILE, w), lambda i: (i, 0))
    head = lambda w: pl.BlockSpec((None, MLA_HEADS, ROW_TILE, w),
                                  lambda i: (i // nsb, 0, i % nsb, 0))
    return pl.pallas_call(
        functools.partial(_ffn_mixin_kernel, tiles_per_batch=nsb),
        grid=(n // ROW_TILE,),
        in_specs=[row(D_MODEL), *_ffn_specs(),
                  row(1), _const_spec((1, LANES)), _const_spec((1, D_MODEL)),
                  _const_spec((D_MODEL, _IN_COLS_EXT)), _const_spec((1, Q_LORA_RANK)),
                  _const_spec((Q_LORA_RANK, MLA_HEADS * QK_PAD)),
                  _const_spec((1, KV_LORA_RANK)),
                  _const_spec((KV_LORA_RANK, MLA_HEADS * QK_NOPE_DIM)),
                  _const_spec((MLA_HEADS * V_HEAD_DIM, KV_LORA_RANK))],
        out_specs=[row(D_MODEL), head(QK_PAD), head(QK_PAD),
                   pl.BlockSpec((None, MLA_HEADS, 1, V_HEAD_DIM, ATTN_TILE),
                                lambda i: (i // nsb, 0, i % nsb, 0, 0)),
                   row(LRU_WIDTH), row(LRU_WIDTH),
                   pl.BlockSpec((None, SUBLANES, LANES), lambda i: (i // nsb, 0, 0))],
        out_shape=[jax.ShapeDtypeStruct((n, D_MODEL), _F32),
                   jax.ShapeDtypeStruct((batch, MLA_HEADS, seq, QK_PAD), _BF16),
                   jax.ShapeDtypeStruct((batch, MLA_HEADS, seq, QK_PAD), _BF16),
                   jax.ShapeDtypeStruct((batch, MLA_HEADS, seq // ATTN_TILE, V_HEAD_DIM,
                                         ATTN_TILE), _BF16),
                   jax.ShapeDtypeStruct((n, LRU_WIDTH), _F32),
                   jax.ShapeDtypeStruct((n, LRU_WIDTH), _F32),
                   jax.ShapeDtypeStruct((batch, SUBLANES, LANES), _F32)],
        compiler_params=_params(("arbitrary",), 56),
        name="ffn_mixer_in",
    )(x, *ffn_args, pos, inv, g_pre, w_in, g_q, w_uq, g_kv, w_k, w_vt)


SHIFT_HEADROOM = 90.0
FIXED_SHIFT_UNROLL = 2
NORM_MARGIN = 1.02


def _attn_kernel(q_ref, k_ref, vt_ref, kmax_ref, o_ref, m_sc, l_sc, acc_sc):
    qi = pl.program_id(1)

    def scores(j, hd):
        k = k_ref[hd, pl.ds(pl.multiple_of(j * ATTN_TILE, ATTN_TILE), ATTN_TILE), :]
        return lax.dot_general(k, q_ref[hd], (((1,), (1,)), ((), ())),
                               preferred_element_type=_F32)

    def run_blocks(blocks, softmax_update):
        items = [(j, hd) for j in blocks for hd in range(MLA_HEADS)]
        s_next = scores(*items[0])
        pending = None
        for n, (j, hd) in enumerate(items):
            s = s_next
            if n + 1 < len(items):
                s_next = scores(*items[n + 1])
            if pending is not None:
                pending()
            pending = softmax_update(j, hd, s)
        pending()

    def diagonal(j, hd, s):
        key = lax.broadcasted_iota(jnp.int32, s.shape, 0)
        qry = lax.broadcasted_iota(jnp.int32, s.shape, 1)
        s = jnp.where(key <= qry, s, NEG_INF)
        m = jnp.max(s, axis=0, keepdims=True)
        p = jnp.exp2(s - m)
        m_sc[hd] = m
        l_sc[hd] = jnp.sum(p, axis=0, keepdims=True)
        p = p.astype(_BF16)

        def values():
            acc_sc[hd] = jnp.dot(vt_ref[hd, j], p, preferred_element_type=_F32)
        return values

    def fixed_shift(j, hd, s):
        p = jnp.exp2(s - m_sc[hd])
        l_sc[hd] = l_sc[hd] + jnp.sum(p, axis=0, keepdims=True)
        p = p.astype(_BF16)

        def values():
            acc_sc[hd] = acc_sc[hd] + jnp.dot(vt_ref[hd, j], p, preferred_element_type=_F32)
        return values

    def running_max(j, hd, s):
        m_old = m_sc[hd]
        m_new = jnp.maximum(m_old, jnp.max(s, axis=0, keepdims=True))
        p = jnp.exp2(s - m_new)
        alpha = jnp.exp2(m_old - m_new)
        l_sc[hd] = alpha * l_sc[hd] + jnp.sum(p, axis=0, keepdims=True)
        m_sc[hd] = m_new
        p = p.astype(_BF16)

        def values():
            pv = jnp.dot(vt_ref[hd, j], p, preferred_element_type=_F32)
            acc_sc[hd] = alpha * acc_sc[hd] + pv
        return values

    run_blocks([qi], diagonal)

    excess = jnp.zeros((1, ATTN_TILE), _F32)
    ones = jnp.ones((2 * SUBLANES, QK_PAD), _BF16)
    for hd in range(MLA_HEADS):
        q = q_ref[hd]
        q_sq = lax.dot_general(ones, q * q, (((1,), (1,)), ((), ())),
                               preferred_element_type=_F32)[0:1]
        k_sq = jnp.max(kmax_ref[hd:hd + 1, :], axis=1, keepdims=True)
        bound = jnp.sqrt(q_sq * k_sq) * NORM_MARGIN
        excess = jnp.maximum(excess, bound - m_sc[hd])
    fixed_ok = jnp.max(excess) <= SHIFT_HEADROOM

    def loop_with(update, unroll):
        def body(t, carry):
            run_blocks([t * unroll + r for r in range(unroll)], update)
            return carry
        lax.fori_loop(0, qi // unroll, body, 0)
        for r in range(unroll - 1, 0, -1):
            @pl.when(qi % unroll >= r)
            def _():
                run_blocks([qi - r], update)

    @pl.when(fixed_ok)
    def _():
        loop_with(fixed_shift, FIXED_SHIFT_UNROLL)

    @pl.when(jnp.logical_not(fixed_ok))
    def _():
        loop_with(running_max, 1)

    for hd in range(MLA_HEADS):
        out = acc_sc[hd] / l_sc[hd]
        o_ref[:, hd * V_HEAD_DIM:(hd + 1) * V_HEAD_DIM] = out.T.astype(o_ref.dtype)


def _attention(q, k, vt, kmax, batch, seq):
    nq = seq // ATTN_TILE
    return pl.pallas_call(
        _attn_kernel,
        grid=(batch, nq),
        in_specs=[pl.BlockSpec((None, MLA_HEADS, ATTN_TILE, QK_PAD), lambda b, i: (b, 0, i, 0)),
                  pl.BlockSpec((None, MLA_HEADS, seq, QK_PAD), lambda b, i: (b, 0, 0, 0),
                               pipeline_mode=pl.Buffered(1)),
                  pl.BlockSpec((None, MLA_HEADS, nq, V_HEAD_DIM, ATTN_TILE),
                               lambda b, i: (b, 0, 0, 0, 0), pipeline_mode=pl.Buffered(1)),
                  pl.BlockSpec((None, SUBLANES, LANES), lambda b, i: (b, 0, 0))],
        out_specs=pl.BlockSpec((None, ATTN_TILE, MLA_WIDTH), lambda b, i: (b, i, 0)),
        out_shape=jax.ShapeDtypeStruct((batch, seq, MLA_WIDTH), _BF16),
        scratch_shapes=[pltpu.VMEM((MLA_HEADS, 1, ATTN_TILE), _F32),
                        pltpu.VMEM((MLA_HEADS, 1, ATTN_TILE), _F32),
                        pltpu.VMEM((MLA_HEADS, V_HEAD_DIM, ATTN_TILE), _F32)],
        compiler_params=_params(("parallel", "arbitrary"), 48),
        name="mla_attention",
    )(q, k, vt, kmax)


def _sigmoid(x):
    return 0.5 * jnp.tanh(0.5 * x) + 0.5


def _linear_recurrence(a, b, h0):
    t, c = a.shape
    groups = t // SUBLANES
    a = a.reshape(groups, SUBLANES, c)
    b = b.reshape(groups, SUBLANES, c)
    sub = lax.broadcasted_iota(jnp.int32, a.shape, 1)
    d = 1
    while d < SUBLANES:
        keep = sub >= d
        b = a * jnp.where(keep, pltpu.roll(b, d, axis=1), 0.0) + b
        a = a * jnp.where(keep, pltpu.roll(a, d, axis=1), 1.0)
        d *= 2
    out = []
    for g in range(groups):
        hg = b[g] + a[g] * h0
        out.append(hg)
        h0 = hg[SUBLANES - 1:SUBLANES, :]
    return jnp.concatenate(out, axis=0)


def _lru_kernel(u_ref, gate_ref, cw_ref, cb_ref, wax_ref, bax_ref, lam_ref, o_ref,
                tail_sc, h_sc):
    @pl.when(pl.program_id(1) == 0)
    def _():
        tail_sc[...] = jnp.zeros(tail_sc.shape, _F32)
        h_sc[...] = jnp.zeros(h_sc.shape, _F32)

    u = u_ref[...]
    tail = tail_sc[...]
    row8 = lax.broadcasted_iota(jnp.int32, tail.shape, 0)
    xc = cb_ref[...] + u * cw_ref[CONV_WIDTH - 1:CONV_WIDTH, :]
    for d in range(1, CONV_WIDTH):
        rolled = pltpu.roll(u, d, axis=0)
        head = jnp.where(row8 < d, pltpu.roll(tail, d, axis=0), rolled[:SUBLANES])
        shifted = jnp.concatenate([head, rolled[SUBLANES:]], axis=0)
        xc = xc + shifted * cw_ref[CONV_WIDTH - 1 - d:CONV_WIDTH - d, :]
    tail_sc[...] = u[u.shape[0] - SUBLANES:, :]

    pre = jnp.dot(xc.astype(_BF16), wax_ref[...], preferred_element_type=_F32) + bax_ref[...]
    r = _sigmoid(pre[:, :LRU_WIDTH])
    i = _sigmoid(pre[:, LRU_WIDTH:])
    neg_lam = -lam_ref[...]
    softplus = jnp.maximum(neg_lam, 0.0) + jnp.log1p(jnp.exp(-jnp.abs(neg_lam)))
    log_a = -LRU_C * r * softplus
    a = jnp.exp(log_a)
    b = jnp.sqrt(-jnp.tanh(log_a) * (a * a + 1.0)) * (i * xc)

    hseq = _linear_recurrence(a, b, h_sc[...])
    h_sc[...] = hseq[hseq.shape[0] - 1:, :]
    o_ref[...] = (hseq * jax.nn.gelu(gate_ref[...])).astype(o_ref.dtype)


def _rglru(u, gate, conv_w, conv_b, w_ax, b_ax, lam, batch, seq):
    nst = seq // LRU_TILE
    row = pl.BlockSpec((LRU_TILE, LRU_WIDTH), lambda b, s: (b * nst + s, 0))
    return pl.pallas_call(
        _lru_kernel,
        grid=(batch, nst),
        in_specs=[row, row, _const_spec((CONV_WIDTH, LRU_WIDTH)), _const_spec((1, LRU_WIDTH)),
                  _const_spec((LRU_WIDTH, 2 * LRU_WIDTH)), _const_spec((1, 2 * LRU_WIDTH)),
                  _const_spec((1, LRU_WIDTH))],
        out_specs=row,
        out_shape=jax.ShapeDtypeStruct((batch * seq, LRU_WIDTH), _BF16),
        scratch_shapes=[pltpu.VMEM((SUBLANES, LRU_WIDTH), _F32), pltpu.VMEM((1, LRU_WIDTH), _F32)],
        compiler_params=_params(("parallel", "arbitrary"), 40),
        name="rglru",
    )(u, gate, conv_w, conv_b, w_ax, b_ax, lam)


def _memkv_kernel(mem_ref, g_ref, w_ref, o_ref):
    m = _rmsnorm(mem_ref[...], g_ref[...]).astype(_BF16)
    o_ref[...] = jnp.dot(m, w_ref[...], preferred_element_type=_F32).astype(o_ref.dtype)


def _memkv(mem, g, w_kv):
    batch, mlen, _ = mem.shape
    return pl.pallas_call(
        _memkv_kernel,
        grid=(batch,),
        in_specs=[pl.BlockSpec((None, mlen, D_MODEL), lambda b: (b, 0, 0)),
                  _const_spec((1, D_MODEL)), _const_spec((D_MODEL, 2 * XA_WIDTH))],
        out_specs=pl.BlockSpec((None, mlen, 2 * XA_WIDTH), lambda b: (b, 0, 0)),
        out_shape=jax.ShapeDtypeStruct((batch, mlen, 2 * XA_WIDTH), _BF16),
        compiler_params=_params(("parallel",), 32),
        name="mem_kv",
    )(mem, g, w_kv)


def _mixout_ffn_kernel(x_ref, ymla_ref, ylru_ref, wout_ref, gmix_ref, gxa_ref, wq_ref, kv_ref,
                       wo_ref, gxo_ref, fgpre_ref, fwgu_ref, fwd_ref, fgpost_ref, o_ref):
    y = jnp.dot(ymla_ref[...], wout_ref[0:MLA_WIDTH, :], preferred_element_type=_F32)
    y = y + jnp.dot(ylru_ref[...], wout_ref[MLA_WIDTH:D_MODEL, :], preferred_element_type=_F32)
    x = x_ref[...] + _rmsnorm(y, gmix_ref[...])

    h = _rmsnorm(x, gxa_ref[...]).astype(_BF16)
    q = jnp.dot(h, wq_ref[...], preferred_element_type=_F32).astype(_BF16)
    scale = 1.0 / math.sqrt(XA_HEAD_DIM)
    heads = []
    for hd in range(XA_HEADS):
        lo = hd * XA_HEAD_DIM
        kh = kv_ref[:, lo:lo + XA_HEAD_DIM]
        vh = kv_ref[:, XA_WIDTH + lo:XA_WIDTH + lo + XA_HEAD_DIM]
        s = lax.dot_general(q[:, lo:lo + XA_HEAD_DIM], kh, (((1,), (1,)), ((), ())),
                            preferred_element_type=_F32) * scale
        e = jnp.exp(s - jnp.max(s, axis=-1, keepdims=True))
        p = (e / jnp.sum(e, axis=-1, keepdims=True)).astype(_BF16)
        heads.append(jnp.dot(p, vh, preferred_element_type=_F32).astype(_BF16))
    o = jnp.concatenate(heads, axis=-1)
    y2 = jnp.dot(o, wo_ref[...], preferred_element_type=_F32)
    x = x + _rmsnorm(y2, gxo_ref[...])
    o_ref[...] = _ffn_body(x, fgpre_ref, fwgu_ref, fwd_ref, fgpost_ref)


def _mixout_ffn(x, y_mla, y_lru, w_out, g_mix, g_xa, w_q, kv, w_o, g_xo, ffn_args, seq):
    n = x.shape[0]
    nsb = seq // ROW_TILE
    mlen = kv.shape[1]
    row = lambda w: pl.BlockSpec((ROW_TILE, w), lambda i: (i, 0))
    return pl.pallas_call(
        _mixout_ffn_kernel,
        grid=(n // ROW_TILE,),
        in_specs=[row(D_MODEL), row(MLA_WIDTH), row(LRU_WIDTH),
                  _const_spec((D_MODEL, D_MODEL)), _const_spec((1, D_MODEL)),
                  _const_spec((1, D_MODEL)), _const_spec((D_MODEL, XA_WIDTH)),
                  pl.BlockSpec((None, mlen, 2 * XA_WIDTH), lambda i: (i // nsb, 0, 0)),
                  _const_spec((XA_WIDTH, D_MODEL)), _const_spec((1, D_MODEL)), *_ffn_specs()],
        out_specs=row(D_MODEL),
        out_shape=jax.ShapeDtypeStruct((n, D_MODEL), _F32),
        compiler_params=_params(("parallel",), 56),
        name="mixer_out_xattn_ffn",
    )(x, y_mla, y_lru, w_out, g_mix, g_xa, w_q, kv, w_o, g_xo, *ffn_args)


def _swap_halves(w):
    half = w.shape[-1] // 2
    return jnp.concatenate([w[..., half:], w[..., :half]], axis=-1)


def _prep_w_in(w_in):
    o1 = Q_LORA_RANK
    o2 = o1 + KV_LORA_RANK
    o3 = o2 + QK_ROPE_DIM
    o4 = o3 + LRU_WIDTH
    k_pe = w_in[:, o2:o3]
    return jnp.concatenate([w_in[:, :o1], w_in[:, o1:o2], w_in[:, o3:o4], w_in[:, o4:],
                            k_pe, _swap_halves(k_pe)], axis=-1).astype(_BF16)


def _prep_w_uq(w_uq):
    w = w_uq.reshape(Q_LORA_RANK, MLA_HEADS, QK_DIM)
    pe = w[..., QK_NOPE_DIM:]
    w = jnp.concatenate([w[..., :QK_NOPE_DIM], pe, _swap_halves(pe)], axis=-1)
    return w.reshape(Q_LORA_RANK, MLA_HEADS * QK_PAD).astype(_BF16)


def _prep_w_ukv(w_ukv):
    w = w_ukv.reshape(KV_LORA_RANK, MLA_HEADS, QK_NOPE_DIM + V_HEAD_DIM)
    w_k = w[..., :QK_NOPE_DIM].reshape(KV_LORA_RANK, MLA_HEADS * QK_NOPE_DIM)
    w_v = w[..., QK_NOPE_DIM:].reshape(KV_LORA_RANK, MLA_HEADS * V_HEAD_DIM)
    return w_k.astype(_BF16), w_v.T.astype(_BF16)


def _block_diag(w):
    nb, d, e = w.shape
    eye = jnp.eye(nb, dtype=w.dtype)
    return (eye[:, None, :, None] * w[:, :, None, :]).reshape(nb * d, nb * e)


def kernel(x, mem, positions, ffn1_pre_g, ffn1_w_gu, ffn1_w_down, ffn1_post_g, mix_pre_g, w_in, q_a_norm_g, w_uq, kv_a_norm_g, w_ukv, conv_w, conv_b, rg_w_a, rg_b_a, rg_w_x, rg_b_x, rg_lambda, w_out, mix_post_g, xa_pre_g, mem_norm_g, xa_w_q, xa_w_kv, xa_w_o, xa_post_g, ffn2_pre_g, ffn2_w_gu, ffn2_w_down, ffn2_post_g):
    batch, seq, _ = x.shape
    n = batch * seq
    depth = ffn1_pre_g.shape[0]
    bf = lambda w: w.astype(_BF16)
    vec = lambda g: g.reshape(1, -1)

    inv = ROPE_THETA ** (-jnp.arange(0, QK_ROPE_DIM, 2, dtype=_F32) / QK_ROPE_DIM)
    inv = jnp.tile(inv, 2 * LANES // QK_ROPE_DIM).reshape(1, LANES)
    pos = positions.reshape(n, 1)

    xf = x.reshape(n, D_MODEL)
    for l in range(depth):
        ffn1 = (vec(ffn1_pre_g[l]), bf(ffn1_w_gu[l]), bf(ffn1_w_down[l]), vec(ffn1_post_g[l]))
        ffn2 = (vec(ffn2_pre_g[l]), bf(ffn2_w_gu[l]), bf(ffn2_w_down[l]), vec(ffn2_post_g[l]))
        w_k, w_vt = _prep_w_ukv(w_ukv[l])
        xf, q, k, vt, u, gate, kmax = _ffn_mixin(
            xf, ffn1, pos, inv, vec(mix_pre_g[l]), _prep_w_in(w_in[l]), vec(q_a_norm_g[l]),
            _prep_w_uq(w_uq[l]), vec(kv_a_norm_g[l]), w_k, w_vt, batch, seq)
        y_mla = _attention(q, k, vt, kmax, batch, seq).reshape(n, MLA_WIDTH)
        w_ax = bf(jnp.concatenate([_block_diag(rg_w_a[l]), _block_diag(rg_w_x[l])], axis=-1))
        b_ax = jnp.concatenate([rg_b_a[l].reshape(1, -1), rg_b_x[l].reshape(1, -1)], axis=-1)
        y_lru = _rglru(u, gate, conv_w[l], vec(conv_b[l]), w_ax, b_ax, vec(rg_lambda[l]),
                       batch, seq)

        kv = _memkv(mem, vec(mem_norm_g[l]), bf(xa_w_kv[l]))
        xf = _mixout_ffn(xf, y_mla, y_lru, bf(w_out[l]), vec(mix_post_g[l]), vec(xa_pre_g[l]),
                         bf(xa_w_q[l]), kv, bf(xa_w_o[l]), vec(xa_post_g[l]), ffn2, seq)
    return xf.reshape(batch, seq, D_MODEL)
```

```python
import functools
import math

import jax
import jax.numpy as jnp
from jax import lax
from jax.experimental import pallas as pl
from jax.experimental.pallas import tpu as pltpu

D_MODEL = 1024
MLA_HEADS = 4
QK_NOPE_DIM = 128
QK_ROPE_DIM = 64
QK_DIM = QK_NOPE_DIM + QK_ROPE_DIM
V_HEAD_DIM = 128
Q_LORA_RANK = 384
KV_LORA_RANK = 256
MLA_WIDTH = MLA_HEADS * V_HEAD_DIM
ROPE_THETA = 10000.0
LRU_WIDTH = D_MODEL - MLA_WIDTH
LRU_BLOCKS = 8
CONV_WIDTH = 4
LRU_C = 8.0
XA_HEADS = 4
XA_HEAD_DIM = 128
XA_WIDTH = XA_HEADS * XA_HEAD_DIM
D_FF = 2816
EPS = 1e-6
NEG_INF = -1e30

LANES = 128
SUBLANES = 8
MXU_DIM = 256
VMEM_BYTES_V7X = 64 * 1024 * 1024

QK_PAD = 2 * LANES

ROW_TILE = 512
OUT_ROW_TILE = 1024
ATTN_TILE = 512
LRU_TILE = 512
FFN_CHUNKS = (768, 768, 768, 512)
assert sum(FFN_CHUNKS) == D_FF and all(c % MXU_DIM == 0 for c in FFN_CHUNKS)

_BF16 = jnp.bfloat16
_F32 = jnp.float32


def _rmsnorm(x, g):
    return x * lax.rsqrt(jnp.mean(x * x, axis=-1, keepdims=True) + EPS) * g


def _const_spec(shape):
    nd = len(shape)
    return pl.BlockSpec(shape, lambda *_: (0,) * nd, pipeline_mode=pl.Buffered(1))


def _params(semantics, vmem_mb):
    return pltpu.CompilerParams(dimension_semantics=semantics,
                                vmem_limit_bytes=vmem_mb * 1024 * 1024)


def _ffn_body(x, gpre_ref, wgu_ref, wd_ref, gpost_ref):
    h = _rmsnorm(x, gpre_ref[...]).astype(_BF16)
    acc = None
    off = 0
    for c in FFN_CHUNKS:
        g = jnp.dot(h, wgu_ref[:, off:off + c], preferred_element_type=_F32)
        u = jnp.dot(h, wgu_ref[:, D_FF + off:D_FF + off + c], preferred_element_type=_F32)
        a = (jax.nn.silu(g) * u).astype(_BF16)
        part = jnp.dot(a, wd_ref[off:off + c, :], preferred_element_type=_F32)
        acc = part if acc is None else acc + part
        off += c
    return x + 0.5 * _rmsnorm(acc, gpost_ref[...])


def _ffn_specs():
    return [_const_spec((1, D_MODEL)), _const_spec((D_MODEL, 2 * D_FF)),
            _const_spec((D_FF, D_MODEL)), _const_spec((1, D_MODEL))]


_EXP2_SCALE = (1.0 / math.sqrt(QK_DIM)) * math.log2(math.e)

_C_Q = 0
_C_KV = _C_Q + Q_LORA_RANK
_C_U = _C_KV + KV_LORA_RANK
_C_GATE = _C_U + LRU_WIDTH
_C_KPE = _C_GATE + LRU_WIDTH
_IN_COLS_EXT = _C_KPE + 2 * QK_ROPE_DIM


def _ffn_mixin_kernel(x_ref, fgpre_ref, fwgu_ref, fwd_ref, fgpost_ref,
                      pos_ref, inv_ref, gpre_ref, win_ref, gq_ref, wuq_ref, gkv_ref,
                      wk_ref, wvt_ref, x1_ref, q_ref, k_ref, vt_ref, u_ref, gate_ref, kmax_ref, *,
                      tiles_per_batch):
    @pl.when(pl.program_id(0) % tiles_per_batch == 0)
    def _():
        kmax_ref[...] = jnp.zeros(kmax_ref.shape, _F32)

    x1 = _ffn_body(x_ref[...], fgpre_ref, fwgu_ref, fwd_ref, fgpost_ref)
    x1_ref[...] = x1
    h = _rmsnorm(x1, gpre_ref[...]).astype(_BF16)
    z = jnp.dot(h, win_ref[...], preferred_element_type=_F32)
    u_ref[...] = z[:, _C_U:_C_GATE]
    gate_ref[...] = z[:, _C_GATE:_C_KPE]

    ang = pos_ref[...].astype(_F32) * inv_ref[...]
    lane = lax.broadcasted_iota(jnp.int32, ang.shape, 1)
    sin = jnp.sin(ang)
    table = jnp.where(lane < QK_ROPE_DIM, jnp.cos(ang),
                      jnp.where(lane < QK_ROPE_DIM + QK_ROPE_DIM // 2, -sin, sin))
    low = (lane < QK_ROPE_DIM).astype(_F32)

    def rope(pair):
        prod = pair * table
        return prod + pltpu.roll(prod, QK_ROPE_DIM, axis=1)

    k_rot = rope(z[:, _C_KPE:_IN_COLS_EXT])
    k_rot_sq = jnp.sum(k_rot * k_rot, axis=1, keepdims=True)
    k_rot = k_rot.astype(_BF16)

    c_q = _rmsnorm(z[:, _C_Q:_C_KV], gq_ref[...]).astype(_BF16)
    q_all = jnp.dot(c_q, wuq_ref[...], preferred_element_type=_F32) * _EXP2_SCALE
    c_kv = _rmsnorm(z[:, _C_KV:_C_U], gkv_ref[...]).astype(_BF16)
    k_nope = jnp.dot(c_kv, wk_ref[...], preferred_element_type=_F32)
    v_t = lax.dot_general(wvt_ref[...], c_kv, (((1,), (1,)), ((), ())),
                          preferred_element_type=_F32)
    for hd in range(MLA_HEADS):
        qh = q_all[:, hd * QK_PAD:(hd + 1) * QK_PAD]
        q_ref[hd, :, 0:LANES] = qh[:, 0:LANES].astype(_BF16)
        q_ref[hd, :, LANES:QK_PAD] = (rope(qh[:, LANES:QK_PAD]) * low).astype(_BF16)
        kh = k_nope[:, hd * LANES:(hd + 1) * LANES]
        k_ref[hd, :, 0:LANES] = kh.astype(_BF16)
        k_ref[hd, :, LANES:QK_PAD] = k_rot
        vt_ref[hd, 0] = v_t[hd * V_HEAD_DIM:(hd + 1) * V_HEAD_DIM, :].astype(_BF16)
        k_sq = jnp.max(jnp.sum(kh * kh, axis=1, keepdims=True) + k_rot_sq, axis=0, keepdims=True)
        kmax_ref[hd:hd + 1, :] = jnp.maximum(kmax_ref[hd:hd + 1, :], k_sq)


def _ffn_mixin(x, ffn_args, pos, inv, g_pre, w_in, g_q, w_uq, g_kv, w_k, w_vt, batch, seq):
    n = x.shape[0]
    nsb = seq // ROW_TILE
    assert ROW_TILE == ATTN_TILE
    row = lambda w: pl.BlockSpec((ROW_TILE, w), lambda i: (i, 0))
    head = lambda w: pl.BlockSpec((None, MLA_HEADS, ROW_TILE, w),
                                  lambda i: (i // nsb, 0, i % nsb, 0))
    return pl.pallas_call(
        functools.partial(_ffn_mixin_kernel, tiles_per_batch=nsb),
        grid=(n // ROW_TILE,),
        in_specs=[row(D_MODEL), *_ffn_specs(),
                  row(1), _const_spec((1, LANES)), _const_spec((1, D_MODEL)),
                  _const_spec((D_MODEL, _IN_COLS_EXT)), _const_spec((1, Q_LORA_RANK)),
                  _const_spec((Q_LORA_RANK, MLA_HEADS * QK_PAD)),
                  _const_spec((1, KV_LORA_RANK)),
                  _const_spec((KV_LORA_RANK, MLA_HEADS * QK_NOPE_DIM)),
                  _const_spec((MLA_HEADS * V_HEAD_DIM, KV_LORA_RANK))],
        out_specs=[row(D_MODEL), head(QK_PAD), head(QK_PAD),
                   pl.BlockSpec((None, MLA_HEADS, 1, V_HEAD_DIM, ATTN_TILE),
                                lambda i: (i // nsb, 0, i % nsb, 0, 0)),
                   row(LRU_WIDTH), row(LRU_WIDTH),
                   pl.BlockSpec((None, SUBLANES, LANES), lambda i: (i // nsb, 0, 0))],
        out_shape=[jax.ShapeDtypeStruct((n, D_MODEL), _F32),
                   jax.ShapeDtypeStruct((batch, MLA_HEADS, seq, QK_PAD), _BF16),
                   jax.ShapeDtypeStruct((batch, MLA_HEADS, seq, QK_PAD), _BF16),
                   jax.ShapeDtypeStruct((batch, MLA_HEADS, seq // ATTN_TILE, V_HEAD_DIM,
                                         ATTN_TILE), _BF16),
                   jax.ShapeDtypeStruct((n, LRU_WIDTH), _F32),
                   jax.ShapeDtypeStruct((n, LRU_WIDTH), _F32),
                   jax.ShapeDtypeStruct((batch, SUBLANES, LANES), _F32)],
        compiler_params=_params(("arbitrary",), 56),
        name="ffn_mixer_in",
    )(x, *ffn_args, pos, inv, g_pre, w_in, g_q, w_uq, g_kv, w_k, w_vt)


SHIFT_HEADROOM = 90.0
FIXED_SHIFT_WIDTH = 4
NORM_MARGIN = 1.02


def _attn_kernel(q_ref, k_ref, vt_ref, kmax_ref, o_ref, m_sc, l_sc, acc_sc):
    qi = pl.program_id(1)

    def scores(j, nb, hd):
        k = k_ref[hd, pl.ds(pl.multiple_of(j * ATTN_TILE, ATTN_TILE), nb * ATTN_TILE), :]
        return lax.dot_general(k, q_ref[hd], (((1,), (1,)), ((), ())),
                               preferred_element_type=_F32)

    def values_t(j, nb, hd):
        return jnp.concatenate([vt_ref[hd, j + r] for r in range(nb)], axis=1)

    def run_items(items, softmax_update):
        s_next = scores(*items[0])
        pending = None
        for n, item in enumerate(items):
            s = s_next
            if n + 1 < len(items):
                s_next = scores(*items[n + 1])
            if pending is not None:
                pending()
            pending = softmax_update(*item, s)
        pending()

    def all_heads(j, nb):
        return [(j, nb, hd) for hd in range(MLA_HEADS)]

    def diagonal(j, nb, hd, s):
        key = lax.broadcasted_iota(jnp.int32, s.shape, 0)
        qry = lax.broadcasted_iota(jnp.int32, s.shape, 1)
        s = jnp.where(key <= qry, s, NEG_INF)
        m = jnp.max(s, axis=0, keepdims=True)
        p = jnp.exp2(s - m)
        m_sc[hd] = m
        l_sc[hd] = jnp.sum(p, axis=0, keepdims=True)
        p = p.astype(_BF16)

        def values():
            acc_sc[hd] = jnp.dot(values_t(j, nb, hd), p, preferred_element_type=_F32)
        return values

    def fixed_shift(j, nb, hd, s):
        p = jnp.exp2(s - m_sc[hd])
        l_sc[hd] = l_sc[hd] + jnp.sum(p, axis=0, keepdims=True)
        p = p.astype(_BF16)

        def values():
            acc_sc[hd] = acc_sc[hd] + jnp.dot(values_t(j, nb, hd), p,
                                              preferred_element_type=_F32)
        return values

    def running_max(j, nb, hd, s):
        m_old = m_sc[hd]
        m_new = jnp.maximum(m_old, jnp.max(s, axis=0, keepdims=True))
        p = jnp.exp2(s - m_new)
        alpha = jnp.exp2(m_old - m_new)
        l_sc[hd] = alpha * l_sc[hd] + jnp.sum(p, axis=0, keepdims=True)
        m_sc[hd] = m_new
        p = p.astype(_BF16)

        def values():
            pv = jnp.dot(values_t(j, nb, hd), p, preferred_element_type=_F32)
            acc_sc[hd] = alpha * acc_sc[hd] + pv
        return values

    run_items(all_heads(qi, 1), diagonal)

    excess = jnp.zeros((1, ATTN_TILE), _F32)
    ones = jnp.ones((2 * SUBLANES, QK_PAD), _BF16)
    for hd in range(MLA_HEADS):
        q = q_ref[hd]
        q_sq = lax.dot_general(ones, q * q, (((1,), (1,)), ((), ())),
                               preferred_element_type=_F32)[0:1]
        k_sq = jnp.max(kmax_ref[hd:hd + 1, :], axis=1, keepdims=True)
        bound = jnp.sqrt(q_sq * k_sq) * NORM_MARGIN
        excess = jnp.maximum(excess, bound - m_sc[hd])
    fixed_ok = jnp.max(excess) <= SHIFT_HEADROOM

    def loop_with(update, width):
        def body(t, carry):
            run_items(all_heads(t * width, width), update)
            return carry
        lax.fori_loop(0, qi // width, body, 0)
        part = width // 2
        while part >= 1:
            @pl.when(qi % (2 * part) >= part)
            def _(part=part):
                run_items(all_heads(qi - qi % (2 * part), part), update)
            part //= 2

    @pl.when(fixed_ok)
    def _():
        loop_with(fixed_shift, FIXED_SHIFT_WIDTH)

    @pl.when(jnp.logical_not(fixed_ok))
    def _():
        loop_with(running_max, 1)

    for hd in range(MLA_HEADS):
        out = acc_sc[hd] / l_sc[hd]
        o_ref[:, hd * V_HEAD_DIM:(hd + 1) * V_HEAD_DIM] = out.T.astype(o_ref.dtype)


def _attention(q, k, vt, kmax, batch, seq):
    nq = seq // ATTN_TILE
    return pl.pallas_call(
        _attn_kernel,
        grid=(batch, nq),
        in_specs=[pl.BlockSpec((None, MLA_HEADS, ATTN_TILE, QK_PAD), lambda b, i: (b, 0, i, 0)),
                  pl.BlockSpec((None, MLA_HEADS, seq, QK_PAD), lambda b, i: (b, 0, 0, 0),
                               pipeline_mode=pl.Buffered(1)),
                  pl.BlockSpec((None, MLA_HEADS, nq, V_HEAD_DIM, ATTN_TILE),
                               lambda b, i: (b, 0, 0, 0, 0), pipeline_mode=pl.Buffered(1)),
                  pl.BlockSpec((None, SUBLANES, LANES), lambda b, i: (b, 0, 0))],
        out_specs=pl.BlockSpec((None, ATTN_TILE, MLA_WIDTH), lambda b, i: (b, i, 0)),
        out_shape=jax.ShapeDtypeStruct((batch, seq, MLA_WIDTH), _BF16),
        scratch_shapes=[pltpu.VMEM((MLA_HEADS, 1, ATTN_TILE), _F32),
                        pltpu.VMEM((MLA_HEADS, 1, ATTN_TILE), _F32),
                        pltpu.VMEM((MLA_HEADS, V_HEAD_DIM, ATTN_TILE), _F32)],
        compiler_params=_params(("parallel", "arbitrary"), 48),
        name="mla_attention",
    )(q, k, vt, kmax)


def _sigmoid(x):
    return 0.5 * jnp.tanh(0.5 * x) + 0.5


def _linear_recurrence(a, b, h0):
    t, c = a.shape
    groups = t // SUBLANES
    a = a.reshape(groups, SUBLANES, c)
    b = b.reshape(groups, SUBLANES, c)
    sub = lax.broadcasted_iota(jnp.int32, a.shape, 1)
    d = 1
    while d < SUBLANES:
        keep = sub >= d
        b = a * jnp.where(keep, pltpu.roll(b, d, axis=1), 0.0) + b
        a = a * jnp.where(keep, pltpu.roll(a, d, axis=1), 1.0)
        d *= 2
    out = []
    for g in range(groups):
        hg = b[g] + a[g] * h0
        out.append(hg)
        h0 = hg[SUBLANES - 1:SUBLANES, :]
    return jnp.concatenate(out, axis=0)


def _lru_kernel(u_ref, gate_ref, cw_ref, cb_ref, wax_ref, bax_ref, lam_ref, o_ref,
                tail_sc, h_sc):
    @pl.when(pl.program_id(1) == 0)
    def _():
        tail_sc[...] = jnp.zeros(tail_sc.shape, _F32)
        h_sc[...] = jnp.zeros(h_sc.shape, _F32)

    u = u_ref[...]
    tail = tail_sc[...]
    row8 = lax.broadcasted_iota(jnp.int32, tail.shape, 0)
    xc = cb_ref[...] + u * cw_ref[CONV_WIDTH - 1:CONV_WIDTH, :]
    for d in range(1, CONV_WIDTH):
        rolled = pltpu.roll(u, d, axis=0)
        head = jnp.where(row8 < d, pltpu.roll(tail, d, axis=0), rolled[:SUBLANES])
        shifted = jnp.concatenate([head, rolled[SUBLANES:]], axis=0)
        xc = xc + shifted * cw_ref[CONV_WIDTH - 1 - d:CONV_WIDTH - d, :]
    tail_sc[...] = u[u.shape[0] - SUBLANES:, :]

    pre = jnp.dot(xc.astype(_BF16), wax_ref[...], preferred_element_type=_F32) + bax_ref[...]
    r = _sigmoid(pre[:, :LRU_WIDTH])
    i = _sigmoid(pre[:, LRU_WIDTH:])
    neg_lam = -lam_ref[...]
    softplus = jnp.maximum(neg_lam, 0.0) + jnp.log1p(jnp.exp(-jnp.abs(neg_lam)))
    log_a = -LRU_C * r * softplus
    a = jnp.exp(log_a)
    b = jnp.sqrt(-jnp.tanh(log_a) * (a * a + 1.0)) * (i * xc)

    hseq = _linear_recurrence(a, b, h_sc[...])
    h_sc[...] = hseq[hseq.shape[0] - 1:, :]
    o_ref[...] = (hseq * jax.nn.gelu(gate_ref[...])).astype(o_ref.dtype)


def _rglru(u, gate, conv_w, conv_b, w_ax, b_ax, lam, batch, seq):
    nst = seq // LRU_TILE
    row = pl.BlockSpec((LRU_TILE, LRU_WIDTH), lambda b, s: (b * nst + s, 0))
    return pl.pallas_call(
        _lru_kernel,
        grid=(batch, nst),
        in_specs=[row, row, _const_spec((CONV_WIDTH, LRU_WIDTH)), _const_spec((1, LRU_WIDTH)),
                  _const_spec((LRU_WIDTH, 2 * LRU_WIDTH)), _const_spec((1, 2 * LRU_WIDTH)),
                  _const_spec((1, LRU_WIDTH))],
        out_specs=row,
        out_shape=jax.ShapeDtypeStruct((batch * seq, LRU_WIDTH), _BF16),
        scratch_shapes=[pltpu.VMEM((SUBLANES, LRU_WIDTH), _F32), pltpu.VMEM((1, LRU_WIDTH), _F32)],
        compiler_params=_params(("parallel", "arbitrary"), 40),
        name="rglru",
    )(u, gate, conv_w, conv_b, w_ax, b_ax, lam)


def _memkv_kernel(mem_ref, g_ref, w_ref, o_ref):
    m = _rmsnorm(mem_ref[...], g_ref[...]).astype(_BF16)
    o_ref[...] = jnp.dot(m, w_ref[...], preferred_element_type=_F32).astype(o_ref.dtype)


def _memkv(mem, g, w_kv):
    batch, mlen, _ = mem.shape
    return pl.pallas_call(
        _memkv_kernel,
        grid=(batch,),
        in_specs=[pl.BlockSpec((None, mlen, D_MODEL), lambda b: (b, 0, 0)),
                  _const_spec((1, D_MODEL)), _const_spec((D_MODEL, 2 * XA_WIDTH))],
        out_specs=pl.BlockSpec((None, mlen, 2 * XA_WIDTH), lambda b: (b, 0, 0)),
        out_shape=jax.ShapeDtypeStruct((batch, mlen, 2 * XA_WIDTH), _BF16),
        compiler_params=_params(("parallel",), 32),
        name="mem_kv",
    )(mem, g, w_kv)


def _mixout_ffn_kernel(x_ref, ymla_ref, ylru_ref, wout_ref, gmix_ref, gxa_ref, wq_ref, kv_ref,
                       wo_ref, gxo_ref, fgpre_ref, fwgu_ref, fwd_ref, fgpost_ref, o_ref):
    y = jnp.dot(ymla_ref[...], wout_ref[0:MLA_WIDTH, :], preferred_element_type=_F32)
    y = y + jnp.dot(ylru_ref[...], wout_ref[MLA_WIDTH:D_MODEL, :], preferred_element_type=_F32)
    x = x_ref[...] + _rmsnorm(y, gmix_ref[...])

    h = _rmsnorm(x, gxa_ref[...]).astype(_BF16)
    q = jnp.dot(h, wq_ref[...], preferred_element_type=_F32).astype(_BF16)
    scale = 1.0 / math.sqrt(XA_HEAD_DIM)
    heads = []
    for hd in range(XA_HEADS):
        lo = hd * XA_HEAD_DIM
        kh = kv_ref[:, lo:lo + XA_HEAD_DIM]
        vh = kv_ref[:, XA_WIDTH + lo:XA_WIDTH + lo + XA_HEAD_DIM]
        s = lax.dot_general(q[:, lo:lo + XA_HEAD_DIM], kh, (((1,), (1,)), ((), ())),
                            preferred_element_type=_F32) * scale
        e = jnp.exp(s - jnp.max(s, axis=-1, keepdims=True))
        p = (e / jnp.sum(e, axis=-1, keepdims=True)).astype(_BF16)
        heads.append(jnp.dot(p, vh, preferred_element_type=_F32).astype(_BF16))
    o = jnp.concatenate(heads, axis=-1)
    y2 = jnp.dot(o, wo_ref[...], preferred_element_type=_F32)
    x = x + _rmsnorm(y2, gxo_ref[...])
    o_ref[...] = _ffn_body(x, fgpre_ref, fwgu_ref, fwd_ref, fgpost_ref)


def _mixout_ffn(x, y_mla, y_lru, w_out, g_mix, g_xa, w_q, kv, w_o, g_xo, ffn_args, seq):
    n = x.shape[0]
    nsb = seq // OUT_ROW_TILE
    mlen = kv.shape[1]
    row = lambda w: pl.BlockSpec((OUT_ROW_TILE, w), lambda i: (i, 0))
    return pl.pallas_call(
        _mixout_ffn_kernel,
        grid=(n // OUT_ROW_TILE,),
        in_specs=[row(D_MODEL), row(MLA_WIDTH), row(LRU_WIDTH),
                  _const_spec((D_MODEL, D_MODEL)), _const_spec((1, D_MODEL)),
                  _const_spec((1, D_MODEL)), _const_spec((D_MODEL, XA_WIDTH)),
                  pl.BlockSpec((None, mlen, 2 * XA_WIDTH), lambda i: (i // nsb, 0, 0)),
                  _const_spec((XA_WIDTH, D_MODEL)), _const_spec((1, D_MODEL)), *_ffn_specs()],
        out_specs=row(D_MODEL),
        out_shape=jax.ShapeDtypeStruct((n, D_MODEL), _F32),
        compiler_params=_params(("parallel",), 56),
        name="mixer_out_xattn_ffn",
    )(x, y_mla, y_lru, w_out, g_mix, g_xa, w_q, kv, w_o, g_xo, *ffn_args)


def _swap_halves(w):
    half = w.shape[-1] // 2
    return jnp.concatenate([w[..., half:], w[..., :half]], axis=-1)


def _prep_w_in(w_in):
    o1 = Q_LORA_RANK
    o2 = o1 + KV_LORA_RANK
    o3 = o2 + QK_ROPE_DIM
    o4 = o3 + LRU_WIDTH
    k_pe = w_in[:, o2:o3]
    return jnp.concatenate([w_in[:, :o1], w_in[:, o1:o2], w_in[:, o3:o4], w_in[:, o4:],
                            k_pe, _swap_halves(k_pe)], axis=-1).astype(_BF16)


def _prep_w_uq(w_uq):
    w = w_uq.reshape(Q_LORA_RANK, MLA_HEADS, QK_DIM)
    pe = w[..., QK_NOPE_DIM:]
    w = jnp.concatenate([w[..., :QK_NOPE_DIM], pe, _swap_halves(pe)], axis=-1)
    return w.reshape(Q_LORA_RANK, MLA_HEADS * QK_PAD).astype(_BF16)


def _prep_w_ukv(w_ukv):
    w = w_ukv.reshape(KV_LORA_RANK, MLA_HEADS, QK_NOPE_DIM + V_HEAD_DIM)
    w_k = w[..., :QK_NOPE_DIM].reshape(KV_LORA_RANK, MLA_HEADS * QK_NOPE_DIM)
    w_v = w[..., QK_NOPE_DIM:].reshape(KV_LORA_RANK, MLA_HEADS * V_HEAD_DIM)
    return w_k.astype(_BF16), w_v.T.astype(_BF16)


def _block_diag(w):
    nb, d, e = w.shape
    eye = jnp.eye(nb, dtype=w.dtype)
    return (eye[:, None, :, None] * w[:, :, None, :]).reshape(nb * d, nb * e)


def kernel(x, mem, positions, ffn1_pre_g, ffn1_w_gu, ffn1_w_down, ffn1_post_g, mix_pre_g, w_in, q_a_norm_g, w_uq, kv_a_norm_g, w_ukv, conv_w, conv_b, rg_w_a, rg_b_a, rg_w_x, rg_b_x, rg_lambda, w_out, mix_post_g, xa_pre_g, mem_norm_g, xa_w_q, xa_w_kv, xa_w_o, xa_post_g, ffn2_pre_g, ffn2_w_gu, ffn2_w_down, ffn2_post_g):
    batch, seq, _ = x.shape
    n = batch * seq
    depth = ffn1_pre_g.shape[0]
    bf = lambda w: w.astype(_BF16)
    vec = lambda g: g.reshape(1, -1)

    inv = ROPE_THETA ** (-jnp.arange(0, QK_ROPE_DIM, 2, dtype=_F32) / QK_ROPE_DIM)
    inv = jnp.tile(inv, 2 * LANES // QK_ROPE_DIM).reshape(1, LANES)
    pos = positions.reshape(n, 1)

    xf = x.reshape(n, D_MODEL)
    for l in range(depth):
        ffn1 = (vec(ffn1_pre_g[l]), bf(ffn1_w_gu[l]), bf(ffn1_w_down[l]), vec(ffn1_post_g[l]))
        ffn2 = (vec(ffn2_pre_g[l]), bf(ffn2_w_gu[l]), bf(ffn2_w_down[l]), vec(ffn2_post_g[l]))
        w_k, w_vt = _prep_w_ukv(w_ukv[l])
        xf, q, k, vt, u, gate, kmax = _ffn_mixin(
            xf, ffn1, pos, inv, vec(mix_pre_g[l]), _prep_w_in(w_in[l]), vec(q_a_norm_g[l]),
            _prep_w_uq(w_uq[l]), vec(kv_a_norm_g[l]), w_k, w_vt, batch, seq)
        y_mla = _attention(q, k, vt, kmax, batch, seq).reshape(n, MLA_WIDTH)
        w_ax = bf(jnp.concatenate([_block_diag(rg_w_a[l]), _block_diag(rg_w_x[l])], axis=-1))
        b_ax = jnp.concatenate([rg_b_a[l].reshape(1, -1), rg_b_x[l].reshape(1, -1)], axis=-1)
        y_lru = _rglru(u, gate, conv_w[l], vec(conv_b[l]), w_ax, b_ax, vec(rg_lambda[l]),
                       batch, seq)

        kv = _memkv(mem, vec(mem_norm_g[l]), bf(xa_w_kv[l]))
        xf = _mixout_ffn(xf, y_mla, y_lru, bf(w_out[l]), vec(mix_post_g[l]), vec(xa_pre_g[l]),
                         bf(xa_w_q[l]), kv, bf(xa_w_o[l]), vec(xa_post_g[l]), ffn2, seq)
    return xf.reshape(batch, seq, D_MODEL)
```

```python
import functools
import math

import jax
import jax.numpy as jnp
from jax import lax
from jax.experimental import pallas as pl
from jax.experimental.pallas import tpu as pltpu

D_MODEL = 1024
MLA_HEADS = 4
QK_NOPE_DIM = 128
QK_ROPE_DIM = 64
QK_DIM = QK_NOPE_DIM + QK_ROPE_DIM
V_HEAD_DIM = 128
Q_LORA_RANK = 384
KV_LORA_RANK = 256
MLA_WIDTH = MLA_HEADS * V_HEAD_DIM
ROPE_THETA = 10000.0
LRU_WIDTH = D_MODEL - MLA_WIDTH
LRU_BLOCKS = 8
CONV_WIDTH = 4
LRU_C = 8.0
XA_HEADS = 4
XA_HEAD_DIM = 128
XA_WIDTH = XA_HEADS * XA_HEAD_DIM
D_FF = 2816
EPS = 1e-6
NEG_INF = -1e30

LANES = 128
SUBLANES = 8
MXU_DIM = 256
VMEM_BYTES_V7X = 64 * 1024 * 1024

QK_PAD = 2 * LANES

ROW_TILE = 512
OUT_ROW_TILE = 1024
ATTN_TILE = 512
LRU_TILE = 512
FFN_CHUNKS = (768, 768, 768, 512)
assert sum(FFN_CHUNKS) == D_FF and all(c % MXU_DIM == 0 for c in FFN_CHUNKS)

_BF16 = jnp.bfloat16
_F32 = jnp.float32


def _rmsnorm(x, g):
    return x * lax.rsqrt(jnp.mean(x * x, axis=-1, keepdims=True) + EPS) * g


def _const_spec(shape):
    nd = len(shape)
    return pl.BlockSpec(shape, lambda *_: (0,) * nd, pipeline_mode=pl.Buffered(1))


def _params(semantics, vmem_mb):
    return pltpu.CompilerParams(dimension_semantics=semantics,
                                vmem_limit_bytes=vmem_mb * 1024 * 1024)


def _ffn_body(x, gpre_ref, wgu_ref, wd_ref, gpost_ref):
    h = _rmsnorm(x, gpre_ref[...]).astype(_BF16)
    acc = None
    off = 0
    for c in FFN_CHUNKS:
        g = jnp.dot(h, wgu_ref[:, off:off + c], preferred_element_type=_F32)
        u = jnp.dot(h, wgu_ref[:, D_FF + off:D_FF + off + c], preferred_element_type=_F32)
        a = (jax.nn.silu(g) * u).astype(_BF16)
        part = jnp.dot(a, wd_ref[off:off + c, :], preferred_element_type=_F32)
        acc = part if acc is None else acc + part
        off += c
    return x + 0.5 * _rmsnorm(acc, gpost_ref[...])


def _ffn_specs():
    return [_const_spec((1, D_MODEL)), _const_spec((D_MODEL, 2 * D_FF)),
            _const_spec((D_FF, D_MODEL)), _const_spec((1, D_MODEL))]


_EXP2_SCALE = (1.0 / math.sqrt(QK_DIM)) * math.log2(math.e)

_C_Q = 0
_C_KV = _C_Q + Q_LORA_RANK
_C_U = _C_KV + KV_LORA_RANK
_C_GATE = _C_U + LRU_WIDTH
_C_KPE = _C_GATE + LRU_WIDTH
_IN_COLS_EXT = _C_KPE + 2 * QK_ROPE_DIM


def _ffn_mixin_kernel(x_ref, fgpre_ref, fwgu_ref, fwd_ref, fgpost_ref,
                      pos_ref, inv_ref, gpre_ref, win_ref, gq_ref, wuq_ref, gkv_ref,
                      wk_ref, wvt_ref, x1_ref, q_ref, k_ref, vt_ref, u_ref, gate_ref, kmax_ref, *,
                      tiles_per_batch):
    @pl.when(pl.program_id(0) % tiles_per_batch == 0)
    def _():
        kmax_ref[...] = jnp.zeros(kmax_ref.shape, _F32)

    x1 = _ffn_body(x_ref[...], fgpre_ref, fwgu_ref, fwd_ref, fgpost_ref)
    x1_ref[...] = x1
    h = _rmsnorm(x1, gpre_ref[...]).astype(_BF16)
    z = jnp.dot(h, win_ref[...], preferred_element_type=_F32)
    u_ref[...] = z[:, _C_U:_C_GATE]
    gate_ref[...] = z[:, _C_GATE:_C_KPE]

    ang = pos_ref[...].astype(_F32) * inv_ref[...]
    lane = lax.broadcasted_iota(jnp.int32, ang.shape, 1)
    sin = jnp.sin(ang)
    table = jnp.where(lane < QK_ROPE_DIM, jnp.cos(ang),
                      jnp.where(lane < QK_ROPE_DIM + QK_ROPE_DIM // 2, -sin, sin))
    low = (lane < QK_ROPE_DIM).astype(_F32)

    def rope(pair):
        prod = pair * table
        return prod + pltpu.roll(prod, QK_ROPE_DIM, axis=1)

    k_rot = rope(z[:, _C_KPE:_IN_COLS_EXT])
    k_rot_sq = jnp.sum(k_rot * k_rot, axis=1, keepdims=True)
    k_rot = k_rot.astype(_BF16)

    c_q = _rmsnorm(z[:, _C_Q:_C_KV], gq_ref[...]).astype(_BF16)
    q_all = jnp.dot(c_q, wuq_ref[...], preferred_element_type=_F32) * _EXP2_SCALE
    c_kv = _rmsnorm(z[:, _C_KV:_C_U], gkv_ref[...]).astype(_BF16)
    k_nope = jnp.dot(c_kv, wk_ref[...], preferred_element_type=_F32)
    v_t = lax.dot_general(wvt_ref[...], c_kv, (((1,), (1,)), ((), ())),
                          preferred_element_type=_F32)
    for hd in range(MLA_HEADS):
        qh = q_all[:, hd * QK_PAD:(hd + 1) * QK_PAD]
        q_ref[hd, :, 0:LANES] = qh[:, 0:LANES].astype(_BF16)
        q_ref[hd, :, LANES:QK_PAD] = (rope(qh[:, LANES:QK_PAD]) * low).astype(_BF16)
        kh = k_nope[:, hd * LANES:(hd + 1) * LANES]
        k_ref[hd, :, 0:LANES] = kh.astype(_BF16)
        k_ref[hd, :, LANES:QK_PAD] = k_rot
        vt_ref[hd, 0] = v_t[hd * V_HEAD_DIM:(hd + 1) * V_HEAD_DIM, :].astype(_BF16)
        k_sq = jnp.max(jnp.sum(kh * kh, axis=1, keepdims=True) + k_rot_sq, axis=0, keepdims=True)
        kmax_ref[hd:hd + 1, :] = jnp.maximum(kmax_ref[hd:hd + 1, :], k_sq)


def _ffn_mixin(x, ffn_args, pos, inv, g_pre, w_in, g_q, w_uq, g_kv, w_k, w_vt, batch, seq):
    n = x.shape[0]
    nsb = seq // ROW_TILE
    assert ROW_TILE == ATTN_TILE
    row = lambda w: pl.BlockSpec((ROW_TILE, w), lambda i: (i, 0))
    head = lambda w: pl.BlockSpec((None, MLA_HEADS, ROW_TILE, w),
                                  lambda i: (i // nsb, 0, i % nsb, 0))
    return pl.pallas_call(
        functools.partial(_ffn_mixin_kernel, tiles_per_batch=nsb),
        grid=(n // ROW_TILE,),
        in_specs=[row(D_MODEL), *_ffn_specs(),
                  row(1), _const_spec((1, LANES)), _const_spec((1, D_MODEL)),
                  _const_spec((D_MODEL, _IN_COLS_EXT)), _const_spec((1, Q_LORA_RANK)),
                  _const_spec((Q_LORA_RANK, MLA_HEADS * QK_PAD)),
                  _const_spec((1, KV_LORA_RANK)),
                  _const_spec((KV_LORA_RANK, MLA_HEADS * QK_NOPE_DIM)),
                  _const_spec((MLA_HEADS * V_HEAD_DIM, KV_LORA_RANK))],
        out_specs=[row(D_MODEL), head(QK_PAD), head(QK_PAD),
                   pl.BlockSpec((None, MLA_HEADS, 1, V_HEAD_DIM, ATTN_TILE),
                                lambda i: (i // nsb, 0, i % nsb, 0, 0)),
                   row(LRU_WIDTH), row(LRU_WIDTH),
                   pl.BlockSpec((None, SUBLANES, LANES), lambda i: (i // nsb, 0, 0))],
        out_shape=[jax.ShapeDtypeStruct((n, D_MODEL), _F32),
                   jax.ShapeDtypeStruct((batch, MLA_HEADS, seq, QK_PAD), _BF16),
                   jax.ShapeDtypeStruct((batch, MLA_HEADS, seq, QK_PAD), _BF16),
                   jax.ShapeDtypeStruct((batch, MLA_HEADS, seq // ATTN_TILE, V_HEAD_DIM,
                                         ATTN_TILE), _BF16),
                   jax.ShapeDtypeStruct((n, LRU_WIDTH), _F32),
                   jax.ShapeDtypeStruct((n, LRU_WIDTH), _F32),
                   jax.ShapeDtypeStruct((batch, SUBLANES, LANES), _F32)],
        compiler_params=_params(("arbitrary",), 56),
        name="ffn_mixer_in",
    )(x, *ffn_args, pos, inv, g_pre, w_in, g_q, w_uq, g_kv, w_k, w_vt)


SHIFT_HEADROOM = 90.0
FIXED_SHIFT_WIDTH = 4
NORM_MARGIN = 1.02


def _attn_kernel(q_ref, k_ref, vt_ref, kmax_ref, o_ref, m_sc, l_sc, acc_sc):
    qi = pl.program_id(1)

    def scores(j, nb, hd):
        k = k_ref[hd, pl.ds(pl.multiple_of(j * ATTN_TILE, ATTN_TILE), nb * ATTN_TILE), :]
        return lax.dot_general(k, q_ref[hd], (((1,), (1,)), ((), ())),
                               preferred_element_type=_F32)

    def values_t(j, nb, hd):
        return jnp.concatenate([vt_ref[hd, j + r] for r in range(nb)], axis=1)

    def run_items(items, softmax_update):
        s_next = scores(*items[0])
        pending = None
        for n, item in enumerate(items):
            s = s_next
            if n + 1 < len(items):
                s_next = scores(*items[n + 1])
            if pending is not None:
                pending()
            pending = softmax_update(*item, s)
        pending()

    def all_heads(j, nb):
        return [(j, nb, hd) for hd in range(MLA_HEADS)]

    def diagonal(j, nb, hd, s):
        key = lax.broadcasted_iota(jnp.int32, s.shape, 0)
        qry = lax.broadcasted_iota(jnp.int32, s.shape, 1)
        s = jnp.where(key <= qry, s, NEG_INF)
        m = jnp.max(s, axis=0, keepdims=True)
        p = jnp.exp2(s - m)
        m_sc[hd] = m
        l_sc[hd] = jnp.sum(p, axis=0, keepdims=True)
        p = p.astype(_BF16)

        def values():
            acc_sc[hd] = jnp.dot(values_t(j, nb, hd), p, preferred_element_type=_F32)
        return values

    def fixed_shift(j, nb, hd, s):
        p = jnp.exp2(s - m_sc[hd])
        l_sc[hd] = l_sc[hd] + jnp.sum(p, axis=0, keepdims=True)
        p = p.astype(_BF16)

        def values():
            acc_sc[hd] = acc_sc[hd] + jnp.dot(values_t(j, nb, hd), p,
                                              preferred_element_type=_F32)
        return values

    def running_max(j, nb, hd, s):
        m_old = m_sc[hd]
        m_new = jnp.maximum(m_old, jnp.max(s, axis=0, keepdims=True))
        p = jnp.exp2(s - m_new)
        alpha = jnp.exp2(m_old - m_new)
        l_sc[hd] = alpha * l_sc[hd] + jnp.sum(p, axis=0, keepdims=True)
        m_sc[hd] = m_new
        p = p.astype(_BF16)

        def values():
            pv = jnp.dot(values_t(j, nb, hd), p, preferred_element_type=_F32)
            acc_sc[hd] = alpha * acc_sc[hd] + pv
        return values

    run_items(all_heads(qi, 1), diagonal)

    excess = jnp.zeros((1, ATTN_TILE), _F32)
    ones = jnp.ones((2 * SUBLANES, QK_PAD), _BF16)
    for hd in range(MLA_HEADS):
        q = q_ref[hd]
        q_sq = lax.dot_general(ones, q * q, (((1,), (1,)), ((), ())),
                               preferred_element_type=_F32)[0:1]
        k_sq = jnp.max(kmax_ref[hd:hd + 1, :], axis=1, keepdims=True)
        bound = jnp.sqrt(q_sq * k_sq) * NORM_MARGIN
        excess = jnp.maximum(excess, bound - m_sc[hd])
    fixed_ok = jnp.max(excess) <= SHIFT_HEADROOM

    def loop_with(update, width):
        def body(t, carry):
            run_items(all_heads(t * width, width), update)
            return carry
        lax.fori_loop(0, qi // width, body, 0)
        part = width // 2
        while part >= 1:
            @pl.when(qi % (2 * part) >= part)
            def _(part=part):
                run_items(all_heads(qi - qi % (2 * part), part), update)
            part //= 2

    @pl.when(fixed_ok)
    def _():
        loop_with(fixed_shift, FIXED_SHIFT_WIDTH)

    @pl.when(jnp.logical_not(fixed_ok))
    def _():
        loop_with(running_max, 1)

    for hd in range(MLA_HEADS):
        out = acc_sc[hd] / l_sc[hd]
        o_ref[:, hd * V_HEAD_DIM:(hd + 1) * V_HEAD_DIM] = out.T.astype(o_ref.dtype)


def _attention(q, k, vt, kmax, batch, seq):
    nq = seq // ATTN_TILE
    return pl.pallas_call(
        _attn_kernel,
        grid=(batch, nq),
        in_specs=[pl.BlockSpec((None, MLA_HEADS, ATTN_TILE, QK_PAD), lambda b, i: (b, 0, i, 0)),
                  pl.BlockSpec((None, MLA_HEADS, seq, QK_PAD), lambda b, i: (b, 0, 0, 0),
                               pipeline_mode=pl.Buffered(1)),
                  pl.BlockSpec((None, MLA_HEADS, nq, V_HEAD_DIM, ATTN_TILE),
                               lambda b, i: (b, 0, 0, 0, 0), pipeline_mode=pl.Buffered(1)),
                  pl.BlockSpec((None, SUBLANES, LANES), lambda b, i: (b, 0, 0))],
        out_specs=pl.BlockSpec((None, ATTN_TILE, MLA_WIDTH), lambda b, i: (b, i, 0)),
        out_shape=jax.ShapeDtypeStruct((batch, seq, MLA_WIDTH), _BF16),
        scratch_shapes=[pltpu.VMEM((MLA_HEADS, 1, ATTN_TILE), _F32),
                        pltpu.VMEM((MLA_HEADS, 1, ATTN_TILE), _F32),
                        pltpu.VMEM((MLA_HEADS, V_HEAD_DIM, ATTN_TILE), _F32)],
        compiler_params=_params(("parallel", "arbitrary"), 48),
        name="mla_attention",
    )(q, k, vt, kmax)


def _sigmoid(x):
    return 0.5 * jnp.tanh(0.5 * x) + 0.5


def _linear_recurrence(a, b, h0):
    t, c = a.shape
    groups = t // SUBLANES
    a = a.reshape(groups, SUBLANES, c)
    b = b.reshape(groups, SUBLANES, c)
    sub = lax.broadcasted_iota(jnp.int32, a.shape, 1)
    d = 1
    while d < SUBLANES:
        keep = sub >= d
        b = a * jnp.where(keep, pltpu.roll(b, d, axis=1), 0.0) + b
        a = a * jnp.where(keep, pltpu.roll(a, d, axis=1), 1.0)
        d *= 2
    out = []
    for g in range(groups):
        hg = b[g] + a[g] * h0
        out.append(hg)
        h0 = hg[SUBLANES - 1:SUBLANES, :]
    return jnp.concatenate(out, axis=0)


def _lru_kernel(u_ref, gate_ref, cw_ref, cb_ref, wax_ref, bax_ref, lam_ref, o_ref,
                tail_sc, h_sc):
    @pl.when(pl.program_id(1) == 0)
    def _():
        tail_sc[...] = jnp.zeros(tail_sc.shape, _F32)
        h_sc[...] = jnp.zeros(h_sc.shape, _F32)

    u = u_ref[...]
    tail = tail_sc[...]
    row8 = lax.broadcasted_iota(jnp.int32, tail.shape, 0)
    xc = cb_ref[...] + u * cw_ref[CONV_WIDTH - 1:CONV_WIDTH, :]
    for d in range(1, CONV_WIDTH):
        rolled = pltpu.roll(u, d, axis=0)
        head = jnp.where(row8 < d, pltpu.roll(tail, d, axis=0), rolled[:SUBLANES])
        shifted = jnp.concatenate([head, rolled[SUBLANES:]], axis=0)
        xc = xc + shifted * cw_ref[CONV_WIDTH - 1 - d:CONV_WIDTH - d, :]
    tail_sc[...] = u[u.shape[0] - SUBLANES:, :]

    pre = jnp.dot(xc.astype(_BF16), wax_ref[...], preferred_element_type=_F32) + bax_ref[...]
    r = _sigmoid(pre[:, :LRU_WIDTH])
    i = _sigmoid(pre[:, LRU_WIDTH:])
    neg_lam = -lam_ref[...]
    softplus = jnp.maximum(neg_lam, 0.0) + jnp.log1p(jnp.exp(-jnp.abs(neg_lam)))
    log_a = -LRU_C * r * softplus
    a = jnp.exp(log_a)
    b = jnp.sqrt(-jnp.tanh(log_a) * (a * a + 1.0)) * (i * xc)

    hseq = _linear_recurrence(a, b, h_sc[...])
    h_sc[...] = hseq[hseq.shape[0] - 1:, :]
    o_ref[...] = (hseq * jax.nn.gelu(gate_ref[...])).astype(o_ref.dtype)


def _rglru(u, gate, conv_w, conv_b, w_ax, b_ax, lam, batch, seq):
    nst = seq // LRU_TILE
    row = pl.BlockSpec((LRU_TILE, LRU_WIDTH), lambda b, s: (b * nst + s, 0))
    return pl.pallas_call(
        _lru_kernel,
        grid=(batch, nst),
        in_specs=[row, row, _const_spec((CONV_WIDTH, LRU_WIDTH)), _const_spec((1, LRU_WIDTH)),
                  _const_spec((LRU_WIDTH, 2 * LRU_WIDTH)), _const_spec((1, 2 * LRU_WIDTH)),
                  _const_spec((1, LRU_WIDTH))],
        out_specs=row,
        out_shape=jax.ShapeDtypeStruct((batch * seq, LRU_WIDTH), _BF16),
        scratch_shapes=[pltpu.VMEM((SUBLANES, LRU_WIDTH), _F32), pltpu.VMEM((1, LRU_WIDTH), _F32)],
        compiler_params=_params(("parallel", "arbitrary"), 40),
        name="rglru",
    )(u, gate, conv_w, conv_b, w_ax, b_ax, lam)


def _memkv_kernel(mem_ref, g_ref, w_ref, o_ref):
    m = _rmsnorm(mem_ref[...], g_ref[...]).astype(_BF16)
    o_ref[...] = jnp.dot(m, w_ref[...], preferred_element_type=_F32).astype(o_ref.dtype)


def _memkv(mem, g, w_kv):
    batch, mlen, _ = mem.shape
    return pl.pallas_call(
        _memkv_kernel,
        grid=(batch,),
        in_specs=[pl.BlockSpec((None, mlen, D_MODEL), lambda b: (b, 0, 0)),
                  _const_spec((1, D_MODEL)), _const_spec((D_MODEL, 2 * XA_WIDTH))],
        out_specs=pl.BlockSpec((None, mlen, 2 * XA_WIDTH), lambda b: (b, 0, 0)),
        out_shape=jax.ShapeDtypeStruct((batch, mlen, 2 * XA_WIDTH), _BF16),
        compiler_params=_params(("parallel",), 32),
        name="mem_kv",
    )(mem, g, w_kv)


def _mixout_ffn_stages(rows, x_ref, ymla_ref, ylru_ref, wout_ref, gmix_ref, gxa_ref, wq_ref,
                       kv_ref, wo_ref, gxo_ref, fgpre_ref, fwgu_ref, fwd_ref, fgpost_ref, o_ref):
    y = jnp.dot(ymla_ref[rows, :], wout_ref[0:MLA_WIDTH, :], preferred_element_type=_F32)
    y = y + jnp.dot(ylru_ref[rows, :], wout_ref[MLA_WIDTH:D_MODEL, :],
                    preferred_element_type=_F32)
    yield
    x = x_ref[rows, :] + _rmsnorm(y, gmix_ref[...])
    h = _rmsnorm(x, gxa_ref[...]).astype(_BF16)
    q = jnp.dot(h, wq_ref[...], preferred_element_type=_F32)
    q = (q * ((1.0 / math.sqrt(XA_HEAD_DIM)) * math.log2(math.e))).astype(_BF16)
    heads = []
    for hd in range(XA_HEADS):
        yield
        lo = hd * XA_HEAD_DIM
        kh = kv_ref[:, lo:lo + XA_HEAD_DIM]
        vh = kv_ref[:, XA_WIDTH + lo:XA_WIDTH + lo + XA_HEAD_DIM]
        s = lax.dot_general(q[:, lo:lo + XA_HEAD_DIM], kh, (((1,), (1,)), ((), ())),
                            preferred_element_type=_F32)
        e = jnp.exp2(s - jnp.max(s, axis=-1, keepdims=True))
        inv_sum = 1.0 / jnp.sum(e, axis=-1, keepdims=True)
        oh = jnp.dot(e.astype(_BF16), vh, preferred_element_type=_F32) * inv_sum
        heads.append(oh.astype(_BF16))
    yield
    o = jnp.concatenate(heads, axis=-1)
    y2 = jnp.dot(o, wo_ref[...], preferred_element_type=_F32)
    yield
    x = x + _rmsnorm(y2, gxo_ref[...])
    h = _rmsnorm(x, fgpre_ref[...]).astype(_BF16)
    acc = None
    off = 0
    for c in FFN_CHUNKS:
        g = jnp.dot(h, fwgu_ref[:, off:off + c], preferred_element_type=_F32)
        u = jnp.dot(h, fwgu_ref[:, D_FF + off:D_FF + off + c], preferred_element_type=_F32)
        a = (jax.nn.silu(g) * u).astype(_BF16)
        part = jnp.dot(a, fwd_ref[off:off + c, :], preferred_element_type=_F32)
        acc = part if acc is None else acc + part
        off += c
        yield
    o_ref[rows, :] = x + 0.5 * _rmsnorm(acc, fgpost_ref[...])


def _mixout_ffn_kernel(*refs):
    subtiles = [_mixout_ffn_stages(pl.ds(r, ROW_TILE), *refs)
                for r in range(0, OUT_ROW_TILE, ROW_TILE)]
    while subtiles:
        for gen in list(subtiles):
            if next(gen, StopIteration) is StopIteration:
                subtiles.remove(gen)


def _mixout_ffn(x, y_mla, y_lru, w_out, g_mix, g_xa, w_q, kv, w_o, g_xo, ffn_args, seq):
    n = x.shape[0]
    nsb = seq // OUT_ROW_TILE
    mlen = kv.shape[1]
    row = lambda w: pl.BlockSpec((OUT_ROW_TILE, w), lambda i: (i, 0))
    return pl.pallas_call(
        _mixout_ffn_kernel,
        grid=(n // OUT_ROW_TILE,),
        in_specs=[row(D_MODEL), row(MLA_WIDTH), row(LRU_WIDTH),
                  _const_spec((D_MODEL, D_MODEL)), _const_spec((1, D_MODEL)),
                  _const_spec((1, D_MODEL)), _const_spec((D_MODEL, XA_WIDTH)),
                  pl.BlockSpec((None, mlen, 2 * XA_WIDTH), lambda i: (i // nsb, 0, 0)),
                  _const_spec((XA_WIDTH, D_MODEL)), _const_spec((1, D_MODEL)), *_ffn_specs()],
        out_specs=row(D_MODEL),
        out_shape=jax.ShapeDtypeStruct((n, D_MODEL), _F32),
        compiler_params=_params(("parallel",), 56),
        name="mixer_out_xattn_ffn",
    )(x, y_mla, y_lru, w_out, g_mix, g_xa, w_q, kv, w_o, g_xo, *ffn_args)


def _swap_halves(w):
    half = w.shape[-1] // 2
    return jnp.concatenate([w[..., half:], w[..., :half]], axis=-1)


def _prep_w_in(w_in):
    o1 = Q_LORA_RANK
    o2 = o1 + KV_LORA_RANK
    o3 = o2 + QK_ROPE_DIM
    o4 = o3 + LRU_WIDTH
    k_pe = w_in[:, o2:o3]
    return jnp.concatenate([w_in[:, :o1], w_in[:, o1:o2], w_in[:, o3:o4], w_in[:, o4:],
                            k_pe, _swap_halves(k_pe)], axis=-1).astype(_BF16)


def _prep_w_uq(w_uq):
    w = w_uq.reshape(Q_LORA_RANK, MLA_HEADS, QK_DIM)
    pe = w[..., QK_NOPE_DIM:]
    w = jnp.concatenate([w[..., :QK_NOPE_DIM], pe, _swap_halves(pe)], axis=-1)
    return w.reshape(Q_LORA_RANK, MLA_HEADS * QK_PAD).astype(_BF16)


def _prep_w_ukv(w_ukv):
    w = w_ukv.reshape(KV_LORA_RANK, MLA_HEADS, QK_NOPE_DIM + V_HEAD_DIM)
    w_k = w[..., :QK_NOPE_DIM].reshape(KV_LORA_RANK, MLA_HEADS * QK_NOPE_DIM)
    w_v = w[..., QK_NOPE_DIM:].reshape(KV_LORA_RANK, MLA_HEADS * V_HEAD_DIM)
    return w_k.astype(_BF16), w_v.T.astype(_BF16)


def _block_diag(w):
    nb, d, e = w.shape
    eye = jnp.eye(nb, dtype=w.dtype)
    return (eye[:, None, :, None] * w[:, :, None, :]).reshape(nb * d, nb * e)


def kernel(x, mem, positions, ffn1_pre_g, ffn1_w_gu, ffn1_w_down, ffn1_post_g, mix_pre_g, w_in, q_a_norm_g, w_uq, kv_a_norm_g, w_ukv, conv_w, conv_b, rg_w_a, rg_b_a, rg_w_x, rg_b_x, rg_lambda, w_out, mix_post_g, xa_pre_g, mem_norm_g, xa_w_q, xa_w_kv, xa_w_o, xa_post_g, ffn2_pre_g, ffn2_w_gu, ffn2_w_down, ffn2_post_g):
    batch, seq, _ = x.shape
    n = batch * seq
    depth = ffn1_pre_g.shape[0]
    bf = lambda w: w.astype(_BF16)
    vec = lambda g: g.reshape(1, -1)

    inv = ROPE_THETA ** (-jnp.arange(0, QK_ROPE_DIM, 2, dtype=_F32) / QK_ROPE_DIM)
    inv = jnp.tile(inv, 2 * LANES // QK_ROPE_DIM).reshape(1, LANES)
    pos = positions.reshape(n, 1)

    xf = x.reshape(n, D_MODEL)
    for l in range(depth):
        ffn1 = (vec(ffn1_pre_g[l]), bf(ffn1_w_gu[l]), bf(ffn1_w_down[l]), vec(ffn1_post_g[l]))
        ffn2 = (vec(ffn2_pre_g[l]), bf(ffn2_w_gu[l]), bf(ffn2_w_down[l]), vec(ffn2_post_g[l]))
        w_k, w_vt = _prep_w_ukv(w_ukv[l])
        xf, q, k, vt, u, gate, kmax = _ffn_mixin(
            xf, ffn1, pos, inv, vec(mix_pre_g[l]), _prep_w_in(w_in[l]), vec(q_a_norm_g[l]),
            _prep_w_uq(w_uq[l]), vec(kv_a_norm_g[l]), w_k, w_vt, batch, seq)
        y_mla = _attention(q, k, vt, kmax, batch, seq).reshape(n, MLA_WIDTH)
        w_ax = bf(jnp.concatenate([_block_diag(rg_w_a[l]), _block_diag(rg_w_x[l])], axis=-1))
        b_ax = jnp.concatenate([rg_b_a[l].reshape(1, -1), rg_b_x[l].reshape(1, -1)], axis=-1)
        y_lru = _rglru(u, gate, conv_w[l], vec(conv_b[l]), w_ax, b_ax, vec(rg_lambda[l]),
                       batch, seq)

        kv = _memkv(mem, vec(mem_norm_g[l]), bf(xa_w_kv[l]))
        xf = _mixout_ffn(xf, y_mla, y_lru, bf(w_out[l]), vec(mix_post_g[l]), vec(xa_pre_g[l]),
                         bf(xa_w_q[l]), kv, bf(xa_w_o[l]), vec(xa_post_g[l]), ffn2, seq)
    return xf.reshape(batch, seq, D_MODEL)
```

```python
import functools
import math

import jax
import jax.numpy as jnp
from jax import lax
from jax.experimental import pallas as pl
from jax.experimental.pallas import tpu as pltpu

D_MODEL = 1024
MLA_HEADS = 4
QK_NOPE_DIM = 128
QK_ROPE_DIM = 64
QK_DIM = QK_NOPE_DIM + QK_ROPE_DIM
V_HEAD_DIM = 128
Q_LORA_RANK = 384
KV_LORA_RANK = 256
MLA_WIDTH = MLA_HEADS * V_HEAD_DIM
ROPE_THETA = 10000.0
LRU_WIDTH = D_MODEL - MLA_WIDTH
LRU_BLOCKS = 8
CONV_WIDTH = 4
LRU_C = 8.0
XA_HEADS = 4
XA_HEAD_DIM = 128
XA_WIDTH = XA_HEADS * XA_HEAD_DIM
D_FF = 2816
EPS = 1e-6
NEG_INF = -1e30

LANES = 128
SUBLANES = 8
MXU_DIM = 256
VMEM_BYTES_V7X = 64 * 1024 * 1024

QK_PAD = 2 * LANES

ROW_TILE = 512
BIG_ROW_TILE = 1024
ATTN_TILE = 512
FFN_CHUNKS = (768, 768, 768, 512)
assert sum(FFN_CHUNKS) == D_FF and all(c % MXU_DIM == 0 for c in FFN_CHUNKS)

_BF16 = jnp.bfloat16
_F32 = jnp.float32


def _rmsnorm(x, g):
    return x * lax.rsqrt(jnp.mean(x * x, axis=-1, keepdims=True) + EPS) * g


def _const_spec(shape):
    nd = len(shape)
    return pl.BlockSpec(shape, lambda *_: (0,) * nd, pipeline_mode=pl.Buffered(1))


def _params(semantics, vmem_mb):
    return pltpu.CompilerParams(dimension_semantics=semantics,
                                vmem_limit_bytes=vmem_mb * 1024 * 1024)


def _alternate(make_stages, tile_rows):
    subtiles = [make_stages(n, pl.ds(n * ROW_TILE, ROW_TILE)) for n in range(tile_rows // ROW_TILE)]
    while subtiles:
        for gen in list(subtiles):
            if next(gen, StopIteration) is StopIteration:
                subtiles.remove(gen)


def _ffn_stages(x, gpre_ref, wgu_ref, wd_ref, gpost_ref):
    h = _rmsnorm(x, gpre_ref[...]).astype(_BF16)
    acc = None
    off = 0
    for c in FFN_CHUNKS:
        g = jnp.dot(h, wgu_ref[:, off:off + c], preferred_element_type=_F32)
        u = jnp.dot(h, wgu_ref[:, D_FF + off:D_FF + off + c], preferred_element_type=_F32)
        a = (jax.nn.silu(g) * u).astype(_BF16)
        part = jnp.dot(a, wd_ref[off:off + c, :], preferred_element_type=_F32)
        acc = part if acc is None else acc + part
        off += c
        yield
    return x + 0.5 * _rmsnorm(acc, gpost_ref[...])


def _ffn_specs():
    return [_const_spec((1, D_MODEL)), _const_spec((D_MODEL, 2 * D_FF)),
            _const_spec((D_FF, D_MODEL)), _const_spec((1, D_MODEL))]


def _ffn_kernel(x_ref, gpre_ref, wgu_ref, wd_ref, gpost_ref, o_ref):
    def stages(n, rows):
        o_ref[rows, :] = yield from _ffn_stages(x_ref[rows, :], gpre_ref, wgu_ref, wd_ref,
                                                gpost_ref)
    _alternate(stages, BIG_ROW_TILE)


def _ffn(x, ffn_args):
    n = x.shape[0]
    row = pl.BlockSpec((BIG_ROW_TILE, D_MODEL), lambda i: (i, 0))
    return pl.pallas_call(
        _ffn_kernel,
        grid=(n // BIG_ROW_TILE,),
        in_specs=[row, *_ffn_specs()],
        out_specs=row,
        out_shape=jax.ShapeDtypeStruct((n, D_MODEL), _F32),
        compiler_params=_params(("parallel",), 56),
        name="ffn",
    )(x, *ffn_args)


def _sigmoid(x):
    return 0.5 * jnp.tanh(0.5 * x) + 0.5


def _linear_recurrence(a, b, h0):
    t, c = a.shape
    groups = t // SUBLANES
    a = a.reshape(groups, SUBLANES, c)
    b = b.reshape(groups, SUBLANES, c)
    sub = lax.broadcasted_iota(jnp.int32, a.shape, 1)
    d = 1
    while d < SUBLANES:
        keep = sub >= d
        b = a * jnp.where(keep, pltpu.roll(b, d, axis=1), 0.0) + b
        a = a * jnp.where(keep, pltpu.roll(a, d, axis=1), 1.0)
        d *= 2
    out = []
    for g in range(groups):
        hg = b[g] + a[g] * h0
        out.append(hg)
        h0 = hg[SUBLANES - 1:SUBLANES, :]
    return jnp.concatenate(out, axis=0)


def _rglru(u, gate, cw_ref, cb_ref, wax_ref, bax_ref, lam_ref, tail_sc, h_sc):
    tail = tail_sc[...]
    row8 = lax.broadcasted_iota(jnp.int32, tail.shape, 0)
    xc = cb_ref[...] + u * cw_ref[CONV_WIDTH - 1:CONV_WIDTH, :]
    for d in range(1, CONV_WIDTH):
        rolled = pltpu.roll(u, d, axis=0)
        head = jnp.where(row8 < d, pltpu.roll(tail, d, axis=0), rolled[:SUBLANES])
        shifted = jnp.concatenate([head, rolled[SUBLANES:]], axis=0)
        xc = xc + shifted * cw_ref[CONV_WIDTH - 1 - d:CONV_WIDTH - d, :]
    tail_sc[...] = u[u.shape[0] - SUBLANES:, :]

    pre = jnp.dot(xc.astype(_BF16), wax_ref[...], preferred_element_type=_F32) + bax_ref[...]
    r = _sigmoid(pre[:, :LRU_WIDTH])
    i = _sigmoid(pre[:, LRU_WIDTH:])
    neg_lam = -lam_ref[...]
    softplus = jnp.maximum(neg_lam, 0.0) + jnp.log1p(jnp.exp(-jnp.abs(neg_lam)))
    log_a = -LRU_C * r * softplus
    a = jnp.exp(log_a)
    b = jnp.sqrt(-jnp.tanh(log_a) * (a * a + 1.0)) * (i * xc)

    hseq = _linear_recurrence(a, b, h_sc[...])
    h_sc[...] = hseq[hseq.shape[0] - 1:, :]
    return hseq * jax.nn.gelu(gate)


_EXP2_SCALE = (1.0 / math.sqrt(QK_DIM)) * math.log2(math.e)

_C_Q = 0
_C_KV = _C_Q + Q_LORA_RANK
_C_U = _C_KV + KV_LORA_RANK
_C_GATE = _C_U + LRU_WIDTH
_C_KPE = _C_GATE + LRU_WIDTH
_IN_COLS_EXT = _C_KPE + 2 * QK_ROPE_DIM


def _mixin_lru_stages(n, rows, x_ref, pos_ref, inv_ref, gpre_ref, win_ref, gq_ref, wuq_ref,
                      gkv_ref, wk_ref, wvt_ref, cw_ref, cb_ref, wax_ref, bax_ref, lam_ref,
                      q_ref, k_ref, vt_ref, ylru_ref, kmax_ref, tail_sc, h_sc):
    h = _rmsnorm(x_ref[rows, :], gpre_ref[...]).astype(_BF16)
    z = jnp.dot(h, win_ref[...], preferred_element_type=_F32)
    yield

    ang = pos_ref[rows, :].astype(_F32) * inv_ref[...]
    lane = lax.broadcasted_iota(jnp.int32, ang.shape, 1)
    sin = jnp.sin(ang)
    table = jnp.where(lane < QK_ROPE_DIM, jnp.cos(ang),
                      jnp.where(lane < QK_ROPE_DIM + QK_ROPE_DIM // 2, -sin, sin))
    low = (lane < QK_ROPE_DIM).astype(_F32)

    def rope(pair):
        prod = pair * table
        return prod + pltpu.roll(prod, QK_ROPE_DIM, axis=1)

    k_rot = rope(z[:, _C_KPE:_IN_COLS_EXT])
    k_rot_sq = jnp.sum(k_rot * k_rot, axis=1, keepdims=True)
    k_rot = k_rot.astype(_BF16)

    c_q = _rmsnorm(z[:, _C_Q:_C_KV], gq_ref[...]).astype(_BF16)
    q_all = jnp.dot(c_q, wuq_ref[...], preferred_element_type=_F32) * _EXP2_SCALE
    c_kv = _rmsnorm(z[:, _C_KV:_C_U], gkv_ref[...]).astype(_BF16)
    k_nope = jnp.dot(c_kv, wk_ref[...], preferred_element_type=_F32)
    v_t = lax.dot_general(wvt_ref[...], c_kv, (((1,), (1,)), ((), ())),
                          preferred_element_type=_F32)
    yield
    for hd in range(MLA_HEADS):
        qh = q_all[:, hd * QK_PAD:(hd + 1) * QK_PAD]
        q_ref[hd, rows, 0:LANES] = qh[:, 0:LANES].astype(_BF16)
        q_ref[hd, rows, LANES:QK_PAD] = (rope(qh[:, LANES:QK_PAD]) * low).astype(_BF16)
        kh = k_nope[:, hd * LANES:(hd + 1) * LANES]
        k_ref[hd, rows, 0:LANES] = kh.astype(_BF16)
        k_ref[hd, rows, LANES:QK_PAD] = k_rot
        vt_ref[hd, n] = v_t[hd * V_HEAD_DIM:(hd + 1) * V_HEAD_DIM, :].astype(_BF16)
        k_sq = jnp.max(jnp.sum(kh * kh, axis=1, keepdims=True) + k_rot_sq, axis=0, keepdims=True)
        kmax_ref[hd:hd + 1, :] = jnp.maximum(kmax_ref[hd:hd + 1, :], k_sq)
    yield
    ylru_ref[rows, :] = _rglru(z[:, _C_U:_C_GATE], z[:, _C_GATE:_C_KPE], cw_ref, cb_ref, wax_ref,
                               bax_ref, lam_ref, tail_sc, h_sc).astype(ylru_ref.dtype)


def _mixin_lru_kernel(*refs, tiles_per_batch):
    kmax_ref, tail_sc, h_sc = refs[-3:]

    @pl.when(pl.program_id(0) % tiles_per_batch == 0)
    def _():
        kmax_ref[...] = jnp.zeros(kmax_ref.shape, _F32)
        tail_sc[...] = jnp.zeros(tail_sc.shape, _F32)
        h_sc[...] = jnp.zeros(h_sc.shape, _F32)

    _alternate(lambda n, rows: _mixin_lru_stages(n, rows, *refs), BIG_ROW_TILE)


def _mixin_lru(x, pos, inv, g_pre, w_in, g_q, w_uq, g_kv, w_k, w_vt, conv_w, conv_b, w_ax, b_ax,
               lam, batch, seq):
    n = x.shape[0]
    nsb = seq // BIG_ROW_TILE
    assert ROW_TILE == ATTN_TILE
    blocks = BIG_ROW_TILE // ATTN_TILE
    row = lambda w: pl.BlockSpec((BIG_ROW_TILE, w), lambda i: (i, 0))
    head = lambda w: pl.BlockSpec((None, MLA_HEADS, BIG_ROW_TILE, w),
                                  lambda i: (i // nsb, 0, i % nsb, 0))
    return pl.pallas_call(
        functools.partial(_mixin_lru_kernel, tiles_per_batch=nsb),
        grid=(n // BIG_ROW_TILE,),
        in_specs=[row(D_MODEL), row(1), _const_spec((1, LANES)), _const_spec((1, D_MODEL)),
                  _const_spec((D_MODEL, _IN_COLS_EXT)), _const_spec((1, Q_LORA_RANK)),
                  _const_spec((Q_LORA_RANK, MLA_HEADS * QK_PAD)),
                  _const_spec((1, KV_LORA_RANK)),
                  _const_spec((KV_LORA_RANK, MLA_HEADS * QK_NOPE_DIM)),
                  _const_spec((MLA_HEADS * V_HEAD_DIM, KV_LORA_RANK)),
                  _const_spec((CONV_WIDTH, LRU_WIDTH)), _const_spec((1, LRU_WIDTH)),
                  _const_spec((LRU_WIDTH, 2 * LRU_WIDTH)), _const_spec((1, 2 * LRU_WIDTH)),
                  _const_spec((1, LRU_WIDTH))],
        out_specs=[head(QK_PAD), head(QK_PAD),
                   pl.BlockSpec((None, MLA_HEADS, blocks, V_HEAD_DIM, ATTN_TILE),
                                lambda i: (i // nsb, 0, i % nsb, 0, 0)),
                   row(LRU_WIDTH),
                   pl.BlockSpec((None, SUBLANES, LANES), lambda i: (i // nsb, 0, 0))],
        out_shape=[jax.ShapeDtypeStruct((batch, MLA_HEADS, seq, QK_PAD), _BF16),
                   jax.ShapeDtypeStruct((batch, MLA_HEADS, seq, QK_PAD), _BF16),
                   jax.ShapeDtypeStruct((batch, MLA_HEADS, seq // ATTN_TILE, V_HEAD_DIM,
                                         ATTN_TILE), _BF16),
                   jax.ShapeDtypeStruct((n, LRU_WIDTH), _BF16),
                   jax.ShapeDtypeStruct((batch, SUBLANES, LANES), _F32)],
        scratch_shapes=[pltpu.VMEM((SUBLANES, LRU_WIDTH), _F32), pltpu.VMEM((1, LRU_WIDTH), _F32)],
        compiler_params=_params(("arbitrary",), 56),
        name="mixer_in_rglru",
    )(x, pos, inv, g_pre, w_in, g_q, w_uq, g_kv, w_k, w_vt, conv_w, conv_b, w_ax, b_ax, lam)


SHIFT_HEADROOM = 90.0
FIXED_SHIFT_WIDTH = 4
NORM_MARGIN = 1.02


def _attn_kernel(q_ref, k_ref, vt_ref, kmax_ref, o_ref, m_sc, l_sc, acc_sc):
    qi = pl.program_id(1)

    def scores(j, nb, hd):
        k = k_ref[hd, pl.ds(pl.multiple_of(j * ATTN_TILE, ATTN_TILE), nb * ATTN_TILE), :]
        return lax.dot_general(k, q_ref[hd], (((1,), (1,)), ((), ())),
                               preferred_element_type=_F32)

    def values_t(j, nb, hd):
        return jnp.concatenate([vt_ref[hd, j + r] for r in range(nb)], axis=1)

    def run_items(items, softmax_update):
        s_next = scores(*items[0])
        pending = None
        for n, item in enumerate(items):
            s = s_next
            if n + 1 < len(items):
                s_next = scores(*items[n + 1])
            if pending is not None:
                pending()
            pending = softmax_update(*item, s)
        pending()

    def all_heads(j, nb):
        return [(j, nb, hd) for hd in range(MLA_HEADS)]

    def diagonal(j, nb, hd, s):
        key = lax.broadcasted_iota(jnp.int32, s.shape, 0)
        qry = lax.broadcasted_iota(jnp.int32, s.shape, 1)
        s = jnp.where(key <= qry, s, NEG_INF)
        m = jnp.max(s, axis=0, keepdims=True)
        p = jnp.exp2(s - m)
        m_sc[hd] = m
        l_sc[hd] = jnp.sum(p, axis=0, keepdims=True)
        p = p.astype(_BF16)

        def values():
            acc_sc[hd] = jnp.dot(values_t(j, nb, hd), p, preferred_element_type=_F32)
        return values

    def fixed_shift(j, nb, hd, s):
        p = jnp.exp2(s - m_sc[hd])
        l_sc[hd] = l_sc[hd] + jnp.sum(p, axis=0, keepdims=True)
        p = p.astype(_BF16)

        def values():
            acc_sc[hd] = acc_sc[hd] + jnp.dot(values_t(j, nb, hd), p,
                                              preferred_element_type=_F32)
        return values

    def running_max(j, nb, hd, s):
        m_old = m_sc[hd]
        m_new = jnp.maximum(m_old, jnp.max(s, axis=0, keepdims=True))
        p = jnp.exp2(s - m_new)
        alpha = jnp.exp2(m_old - m_new)
        l_sc[hd] = alpha * l_sc[hd] + jnp.sum(p, axis=0, keepdims=True)
        m_sc[hd] = m_new
        p = p.astype(_BF16)

        def values():
            pv = jnp.dot(values_t(j, nb, hd), p, preferred_element_type=_F32)
            acc_sc[hd] = alpha * acc_sc[hd] + pv
        return values

    run_items(all_heads(qi, 1), diagonal)

    excess = jnp.zeros((1, ATTN_TILE), _F32)
    ones = jnp.ones((2 * SUBLANES, QK_PAD), _BF16)
    for hd in range(MLA_HEADS):
        q = q_ref[hd]
        q_sq = lax.dot_general(ones, q * q, (((1,), (1,)), ((), ())),
                               preferred_element_type=_F32)[0:1]
        k_sq = jnp.max(kmax_ref[hd:hd + 1, :], axis=1, keepdims=True)
        bound = jnp.sqrt(q_sq * k_sq) * NORM_MARGIN
        excess = jnp.maximum(excess, bound - m_sc[hd])
    fixed_ok = jnp.max(excess) <= SHIFT_HEADROOM

    def loop_with(update, width):
        def body(t, carry):
            run_items(all_heads(t * width, width), update)
            return carry
        lax.fori_loop(0, qi // width, body, 0)
        part = width // 2
        while part >= 1:
            @pl.when(qi % (2 * part) >= part)
            def _(part=part):
                run_items(all_heads(qi - qi % (2 * part), part), update)
            part //= 2

    @pl.when(fixed_ok)
    def _():
        loop_with(fixed_shift, FIXED_SHIFT_WIDTH)

    @pl.when(jnp.logical_not(fixed_ok))
    def _():
        loop_with(running_max, 1)

    for hd in range(MLA_HEADS):
        out = acc_sc[hd] / l_sc[hd]
        o_ref[:, hd * V_HEAD_DIM:(hd + 1) * V_HEAD_DIM] = out.T.astype(o_ref.dtype)


def _attention(q, k, vt, kmax, batch, seq):
    nq = seq // ATTN_TILE
    return pl.pallas_call(
        _attn_kernel,
        grid=(batch, nq),
        in_specs=[pl.BlockSpec((None, MLA_HEADS, ATTN_TILE, QK_PAD), lambda b, i: (b, 0, i, 0)),
                  pl.BlockSpec((None, MLA_HEADS, seq, QK_PAD), lambda b, i: (b, 0, 0, 0),
                               pipeline_mode=pl.Buffered(1)),
                  pl.BlockSpec((None, MLA_HEADS, nq, V_HEAD_DIM, ATTN_TILE),
                               lambda b, i: (b, 0, 0, 0, 0), pipeline_mode=pl.Buffered(1)),
                  pl.BlockSpec((None, SUBLANES, LANES), lambda b, i: (b, 0, 0))],
        out_specs=pl.BlockSpec((None, ATTN_TILE, MLA_WIDTH), lambda b, i: (b, i, 0)),
        out_shape=jax.ShapeDtypeStruct((batch, seq, MLA_WIDTH), _BF16),
        scratch_shapes=[pltpu.VMEM((MLA_HEADS, 1, ATTN_TILE), _F32),
                        pltpu.VMEM((MLA_HEADS, 1, ATTN_TILE), _F32),
                        pltpu.VMEM((MLA_HEADS, V_HEAD_DIM, ATTN_TILE), _F32)],
        compiler_params=_params(("parallel", "arbitrary"), 48),
        name="mla_attention",
    )(q, k, vt, kmax)


def _memkv_kernel(mem_ref, g_ref, w_ref, o_ref):
    m = _rmsnorm(mem_ref[...], g_ref[...]).astype(_BF16)
    o_ref[...] = jnp.dot(m, w_ref[...], preferred_element_type=_F32).astype(o_ref.dtype)


def _memkv(mem, g, w_kv):
    batch, mlen, _ = mem.shape
    return pl.pallas_call(
        _memkv_kernel,
        grid=(batch,),
        in_specs=[pl.BlockSpec((None, mlen, D_MODEL), lambda b: (b, 0, 0)),
                  _const_spec((1, D_MODEL)), _const_spec((D_MODEL, 2 * XA_WIDTH))],
        out_specs=pl.BlockSpec((None, mlen, 2 * XA_WIDTH), lambda b: (b, 0, 0)),
        out_shape=jax.ShapeDtypeStruct((batch, mlen, 2 * XA_WIDTH), _BF16),
        compiler_params=_params(("parallel",), 32),
        name="mem_kv",
    )(mem, g, w_kv)


def _mixout_ffn_stages(rows, x_ref, ymla_ref, ylru_ref, wout_ref, gmix_ref, gxa_ref, wq_ref,
                       kv_ref, wo_ref, gxo_ref, fgpre_ref, fwgu_ref, fwd_ref, fgpost_ref, o_ref):
    y = jnp.dot(ymla_ref[rows, :], wout_ref[0:MLA_WIDTH, :], preferred_element_type=_F32)
    y = y + jnp.dot(ylru_ref[rows, :], wout_ref[MLA_WIDTH:D_MODEL, :],
                    preferred_element_type=_F32)
    yield
    x = x_ref[rows, :] + _rmsnorm(y, gmix_ref[...])
    h = _rmsnorm(x, gxa_ref[...]).astype(_BF16)
    q = jnp.dot(h, wq_ref[...], preferred_element_type=_F32)
    q = (q * ((1.0 / math.sqrt(XA_HEAD_DIM)) * math.log2(math.e))).astype(_BF16)
    heads = []
    for hd in range(XA_HEADS):
        yield
        lo = hd * XA_HEAD_DIM
        kh = kv_ref[:, lo:lo + XA_HEAD_DIM]
        vh = kv_ref[:, XA_WIDTH + lo:XA_WIDTH + lo + XA_HEAD_DIM]
        s = lax.dot_general(q[:, lo:lo + XA_HEAD_DIM], kh, (((1,), (1,)), ((), ())),
                            preferred_element_type=_F32)
        e = jnp.exp2(s - jnp.max(s, axis=-1, keepdims=True))
        inv_sum = 1.0 / jnp.sum(e, axis=-1, keepdims=True)
        oh = jnp.dot(e.astype(_BF16), vh, preferred_element_type=_F32) * inv_sum
        heads.append(oh.astype(_BF16))
    yield
    o = jnp.concatenate(heads, axis=-1)
    y2 = jnp.dot(o, wo_ref[...], preferred_element_type=_F32)
    yield
    x = x + _rmsnorm(y2, gxo_ref[...])
    o_ref[rows, :] = yield from _ffn_stages(x, fgpre_ref, fwgu_ref, fwd_ref, fgpost_ref)


def _mixout_ffn_kernel(*refs):
    _alternate(lambda n, rows: _mixout_ffn_stages(rows, *refs), BIG_ROW_TILE)


def _mixout_ffn(x, y_mla, y_lru, w_out, g_mix, g_xa, w_q, kv, w_o, g_xo, ffn_args, seq):
    n = x.shape[0]
    nsb = seq // BIG_ROW_TILE
    mlen = kv.shape[1]
    row = lambda w: pl.BlockSpec((BIG_ROW_TILE, w), lambda i: (i, 0))
    return pl.pallas_call(
        _mixout_ffn_kernel,
        grid=(n // BIG_ROW_TILE,),
        in_specs=[row(D_MODEL), row(MLA_WIDTH), row(LRU_WIDTH),
                  _const_spec((D_MODEL, D_MODEL)), _const_spec((1, D_MODEL)),
                  _const_spec((1, D_MODEL)), _const_spec((D_MODEL, XA_WIDTH)),
                  pl.BlockSpec((None, mlen, 2 * XA_WIDTH), lambda i: (i // nsb, 0, 0)),
                  _const_spec((XA_WIDTH, D_MODEL)), _const_spec((1, D_MODEL)), *_ffn_specs()],
        out_specs=row(D_MODEL),
        out_shape=jax.ShapeDtypeStruct((n, D_MODEL), _F32),
        compiler_params=_params(("parallel",), 56),
        name="mixer_out_xattn_ffn",
    )(x, y_mla, y_lru, w_out, g_mix, g_xa, w_q, kv, w_o, g_xo, *ffn_args)


def _swap_halves(w):
    half = w.shape[-1] // 2
    return jnp.concatenate([w[..., half:], w[..., :half]], axis=-1)


def _prep_w_in(w_in):
    o1 = Q_LORA_RANK
    o2 = o1 + KV_LORA_RANK
    o3 = o2 + QK_ROPE_DIM
    o4 = o3 + LRU_WIDTH
    k_pe = w_in[:, o2:o3]
    return jnp.concatenate([w_in[:, :o1], w_in[:, o1:o2], w_in[:, o3:o4], w_in[:, o4:],
                            k_pe, _swap_halves(k_pe)], axis=-1).astype(_BF16)


def _prep_w_uq(w_uq):
    w = w_uq.reshape(Q_LORA_RANK, MLA_HEADS, QK_DIM)
    pe = w[..., QK_NOPE_DIM:]
    w = jnp.concatenate([w[..., :QK_NOPE_DIM], pe, _swap_halves(pe)], axis=-1)
    return w.reshape(Q_LORA_RANK, MLA_HEADS * QK_PAD).astype(_BF16)


def _prep_w_ukv(w_ukv):
    w = w_ukv.reshape(KV_LORA_RANK, MLA_HEADS, QK_NOPE_DIM + V_HEAD_DIM)
    w_k = w[..., :QK_NOPE_DIM].reshape(KV_LORA_RANK, MLA_HEADS * QK_NOPE_DIM)
    w_v = w[..., QK_NOPE_DIM:].reshape(KV_LORA_RANK, MLA_HEADS * V_HEAD_DIM)
    return w_k.astype(_BF16), w_v.T.astype(_BF16)


def _block_diag(w):
    nb, d, e = w.shape
    eye = jnp.eye(nb, dtype=w.dtype)
    return (eye[:, None, :, None] * w[:, :, None, :]).reshape(nb * d, nb * e)


def kernel(x, mem, positions, ffn1_pre_g, ffn1_w_gu, ffn1_w_down, ffn1_post_g, mix_pre_g, w_in, q_a_norm_g, w_uq, kv_a_norm_g, w_ukv, conv_w, conv_b, rg_w_a, rg_b_a, rg_w_x, rg_b_x, rg_lambda, w_out, mix_post_g, xa_pre_g, mem_norm_g, xa_w_q, xa_w_kv, xa_w_o, xa_post_g, ffn2_pre_g, ffn2_w_gu, ffn2_w_down, ffn2_post_g):
    batch, seq, _ = x.shape
    n = batch * seq
    depth = ffn1_pre_g.shape[0]
    bf = lambda w: w.astype(_BF16)
    vec = lambda g: g.reshape(1, -1)

    inv = ROPE_THETA ** (-jnp.arange(0, QK_ROPE_DIM, 2, dtype=_F32) / QK_ROPE_DIM)
    inv = jnp.tile(inv, 2 * LANES // QK_ROPE_DIM).reshape(1, LANES)
    pos = positions.reshape(n, 1)

    xf = x.reshape(n, D_MODEL)
    for l in range(depth):
        ffn1 = (vec(ffn1_pre_g[l]), bf(ffn1_w_gu[l]), bf(ffn1_w_down[l]), vec(ffn1_post_g[l]))
        ffn2 = (vec(ffn2_pre_g[l]), bf(ffn2_w_gu[l]), bf(ffn2_w_down[l]), vec(ffn2_post_g[l]))
        w_k, w_vt = _prep_w_ukv(w_ukv[l])
        w_ax = bf(jnp.concatenate([_block_diag(rg_w_a[l]), _block_diag(rg_w_x[l])], axis=-1))
        b_ax = jnp.concatenate([rg_b_a[l].reshape(1, -1), rg_b_x[l].reshape(1, -1)], axis=-1)
        xf = _ffn(xf, ffn1)
        q, k, vt, y_lru, kmax = _mixin_lru(
            xf, pos, inv, vec(mix_pre_g[l]), _prep_w_in(w_in[l]), vec(q_a_norm_g[l]),
            _prep_w_uq(w_uq[l]), vec(kv_a_norm_g[l]), w_k, w_vt, conv_w[l], vec(conv_b[l]),
            w_ax, b_ax, vec(rg_lambda[l]), batch, seq)
        y_mla = _attention(q, k, vt, kmax, batch, seq).reshape(n, MLA_WIDTH)

        kv = _memkv(mem, vec(mem_norm_g[l]), bf(xa_w_kv[l]))
        xf = _mixout_ffn(xf, y_mla, y_lru, bf(w_out[l]), vec(mix_post_g[l]), vec(xa_pre_g[l]),
                         bf(xa_w_q[l]), kv, bf(xa_w_o[l]), vec(xa_post_g[l]), ffn2, seq)
    return xf.reshape(batch, seq, D_MODEL)
```

```python
import functools
import math

import jax
import jax.numpy as jnp
from jax import lax
from jax.experimental import pallas as pl
from jax.experimental.pallas import tpu as pltpu

D_MODEL = 1024
MLA_HEADS = 4
QK_NOPE_DIM = 128
QK_ROPE_DIM = 64
QK_DIM = QK_NOPE_DIM + QK_ROPE_DIM
V_HEAD_DIM = 128
Q_LORA_RANK = 384
KV_LORA_RANK = 256
MLA_WIDTH = MLA_HEADS * V_HEAD_DIM
ROPE_THETA = 10000.0
LRU_WIDTH = D_MODEL - MLA_WIDTH
LRU_BLOCKS = 8
CONV_WIDTH = 4
LRU_C = 8.0
XA_HEADS = 4
XA_HEAD_DIM = 128
XA_WIDTH = XA_HEADS * XA_HEAD_DIM
D_FF = 2816
EPS = 1e-6
NEG_INF = -1e30

LANES = 128
SUBLANES = 8
MXU_DIM = 256
VMEM_BYTES_V7X = 64 * 1024 * 1024

QK_PAD = 2 * LANES

ROW_TILE = 512
BIG_ROW_TILE = 1024
ATTN_TILE = 512
FFN_CHUNKS = (768, 768, 768, 512)
assert sum(FFN_CHUNKS) == D_FF and all(c % MXU_DIM == 0 for c in FFN_CHUNKS)

_BF16 = jnp.bfloat16
_F32 = jnp.float32


def _rmsnorm(x, g):
    return x * lax.rsqrt(jnp.mean(x * x, axis=-1, keepdims=True) + EPS) * g


def _const_spec(shape):
    nd = len(shape)
    return pl.BlockSpec(shape, lambda *_: (0,) * nd, pipeline_mode=pl.Buffered(1))


def _params(semantics, vmem_mb):
    return pltpu.CompilerParams(dimension_semantics=semantics,
                                vmem_limit_bytes=vmem_mb * 1024 * 1024)


def _alternate(make_stages, tile_rows):
    subtiles = [make_stages(n, pl.ds(n * ROW_TILE, ROW_TILE)) for n in range(tile_rows // ROW_TILE)]
    while subtiles:
        for gen in list(subtiles):
            if next(gen, StopIteration) is StopIteration:
                subtiles.remove(gen)


def _ffn_stages(x, gpre_ref, wgu_ref, wd_ref, gpost_ref):
    h = _rmsnorm(x, gpre_ref[...]).astype(_BF16)
    acc = None
    off = 0
    for c in FFN_CHUNKS:
        g = jnp.dot(h, wgu_ref[:, off:off + c], preferred_element_type=_F32)
        u = jnp.dot(h, wgu_ref[:, D_FF + off:D_FF + off + c], preferred_element_type=_F32)
        a = (jax.nn.silu(g) * u).astype(_BF16)
        part = jnp.dot(a, wd_ref[off:off + c, :], preferred_element_type=_F32)
        acc = part if acc is None else acc + part
        off += c
        yield
    return x + 0.5 * _rmsnorm(acc, gpost_ref[...])


def _ffn_specs():
    return [_const_spec((1, D_MODEL)), _const_spec((D_MODEL, 2 * D_FF)),
            _const_spec((D_FF, D_MODEL)), _const_spec((1, D_MODEL))]


def _ffn_kernel(x_ref, gpre_ref, wgu_ref, wd_ref, gpost_ref, o_ref):
    def stages(n, rows):
        o_ref[rows, :] = yield from _ffn_stages(x_ref[rows, :], gpre_ref, wgu_ref, wd_ref,
                                                gpost_ref)
    _alternate(stages, BIG_ROW_TILE)


def _ffn(x, ffn_args):
    n = x.shape[0]
    row = pl.BlockSpec((BIG_ROW_TILE, D_MODEL), lambda i: (i, 0))
    return pl.pallas_call(
        _ffn_kernel,
        grid=(n // BIG_ROW_TILE,),
        in_specs=[row, *_ffn_specs()],
        out_specs=row,
        out_shape=jax.ShapeDtypeStruct((n, D_MODEL), _F32),
        compiler_params=_params(("parallel",), 56),
        name="ffn",
    )(x, *ffn_args)


def _sigmoid(x):
    return 0.5 * jnp.tanh(0.5 * x) + 0.5


def _linear_recurrence(a, b, h0):
    t, c = a.shape
    groups = t // SUBLANES
    a = a.reshape(groups, SUBLANES, c)
    b = b.reshape(groups, SUBLANES, c)
    sub = lax.broadcasted_iota(jnp.int32, a.shape, 1)
    d = 1
    while d < SUBLANES:
        keep = sub >= d
        b = a * jnp.where(keep, pltpu.roll(b, d, axis=1), 0.0) + b
        a = a * jnp.where(keep, pltpu.roll(a, d, axis=1), 1.0)
        d *= 2
    out = []
    for g in range(groups):
        hg = b[g] + a[g] * h0
        out.append(hg)
        h0 = hg[SUBLANES - 1:SUBLANES, :]
    return jnp.concatenate(out, axis=0)


def _rglru(u, gate, cw_ref, cb_ref, wax_ref, bax_ref, lam_ref, tail_sc, h_sc):
    tail = tail_sc[...]
    row8 = lax.broadcasted_iota(jnp.int32, tail.shape, 0)
    xc = cb_ref[...] + u * cw_ref[CONV_WIDTH - 1:CONV_WIDTH, :]
    for d in range(1, CONV_WIDTH):
        rolled = pltpu.roll(u, d, axis=0)
        head = jnp.where(row8 < d, pltpu.roll(tail, d, axis=0), rolled[:SUBLANES])
        shifted = jnp.concatenate([head, rolled[SUBLANES:]], axis=0)
        xc = xc + shifted * cw_ref[CONV_WIDTH - 1 - d:CONV_WIDTH - d, :]
    tail_sc[...] = u[u.shape[0] - SUBLANES:, :]

    pre = jnp.dot(xc.astype(_BF16), wax_ref[...], preferred_element_type=_F32) + bax_ref[...]
    r = _sigmoid(pre[:, :LRU_WIDTH])
    i = _sigmoid(pre[:, LRU_WIDTH:])
    neg_lam = -lam_ref[...]
    softplus = jnp.maximum(neg_lam, 0.0) + jnp.log1p(jnp.exp(-jnp.abs(neg_lam)))
    log_a = -LRU_C * r * softplus
    a = jnp.exp(log_a)
    b = jnp.sqrt(-jnp.tanh(log_a) * (a * a + 1.0)) * (i * xc)

    hseq = _linear_recurrence(a, b, h_sc[...])
    h_sc[...] = hseq[hseq.shape[0] - 1:, :]
    return hseq * jax.nn.gelu(gate)


_EXP2_SCALE = (1.0 / math.sqrt(QK_DIM)) * math.log2(math.e)

_C_Q = 0
_C_KV = _C_Q + Q_LORA_RANK
_C_U = _C_KV + KV_LORA_RANK
_C_GATE = _C_U + LRU_WIDTH
_C_KPE = _C_GATE + LRU_WIDTH
_IN_COLS_EXT = _C_KPE + 2 * QK_ROPE_DIM


def _rope_table(pos_ref, rows, inv):
    freqs = QK_ROPE_DIM // 2
    parts = LANES // freqs
    quarter = rows.size // parts
    lane = lax.broadcasted_iota(jnp.int32, (quarter, LANES), 1)
    pos = None
    for g in range(parts):
        pg = pos_ref[pl.ds(rows.start + g * quarter, quarter), :].astype(_F32)
        pos = pg if pos is None else jnp.where(lane < g * freqs, pos, pg)
    ang = pos * inv
    cos, sin = jnp.cos(ang), jnp.sin(ang)
    out = []
    for g in range(parts):
        def placed(x, t):
            shift = (freqs * (t - g)) % LANES
            return pltpu.roll(x, shift, axis=1) if shift else x
        out.append(jnp.where(lane < freqs, placed(cos, 0),
                             jnp.where(lane < 2 * freqs, placed(cos, 1),
                                       jnp.where(lane < 3 * freqs, -placed(sin, 2),
                                                 placed(sin, 3)))))
    return jnp.concatenate(out, axis=0)


def _mixin_lru_stages(n, rows, x_ref, pos_ref, inv_ref, gpre_ref, win_ref, gq_ref, wuq_ref,
                      gkv_ref, wk_ref, wvt_ref, cw_ref, cb_ref, wax_ref, bax_ref, lam_ref,
                      q_ref, k_ref, vt_ref, ylru_ref, kmax_ref, tail_sc, h_sc):
    h = _rmsnorm(x_ref[rows, :], gpre_ref[...]).astype(_BF16)
    z = jnp.dot(h, win_ref[...], preferred_element_type=_F32)
    yield

    table = _rope_table(pos_ref, rows, inv_ref[...])
    lane = lax.broadcasted_iota(jnp.int32, table.shape, 1)
    low = (lane < QK_ROPE_DIM).astype(_F32)

    def rope(pair):
        prod = pair * table
        return prod + pltpu.roll(prod, QK_ROPE_DIM, axis=1)

    k_rot = rope(z[:, _C_KPE:_IN_COLS_EXT])
    k_rot_sq = jnp.sum(k_rot * k_rot, axis=1, keepdims=True)
    k_rot = k_rot.astype(_BF16)

    c_q = _rmsnorm(z[:, _C_Q:_C_KV], gq_ref[...]).astype(_BF16)
    q_all = jnp.dot(c_q, wuq_ref[...], preferred_element_type=_F32) * _EXP2_SCALE
    c_kv = _rmsnorm(z[:, _C_KV:_C_U], gkv_ref[...]).astype(_BF16)
    k_nope = jnp.dot(c_kv, wk_ref[...], preferred_element_type=_F32)
    v_t = lax.dot_general(wvt_ref[...], c_kv, (((1,), (1,)), ((), ())),
                          preferred_element_type=_F32)
    yield
    for hd in range(MLA_HEADS):
        qh = q_all[:, hd * QK_PAD:(hd + 1) * QK_PAD]
        q_ref[hd, rows, 0:LANES] = qh[:, 0:LANES].astype(_BF16)
        q_ref[hd, rows, LANES:QK_PAD] = (rope(qh[:, LANES:QK_PAD]) * low).astype(_BF16)
        kh = k_nope[:, hd * LANES:(hd + 1) * LANES]
        k_ref[hd, rows, 0:LANES] = kh.astype(_BF16)
        k_ref[hd, rows, LANES:QK_PAD] = k_rot
        vt_ref[hd, n] = v_t[hd * V_HEAD_DIM:(hd + 1) * V_HEAD_DIM, :].astype(_BF16)
        k_sq = jnp.max(jnp.sum(kh * kh, axis=1, keepdims=True) + k_rot_sq, axis=0, keepdims=True)
        kmax_ref[hd:hd + 1, :] = jnp.maximum(kmax_ref[hd:hd + 1, :], k_sq)
    yield
    ylru_ref[rows, :] = _rglru(z[:, _C_U:_C_GATE], z[:, _C_GATE:_C_KPE], cw_ref, cb_ref, wax_ref,
                               bax_ref, lam_ref, tail_sc, h_sc).astype(ylru_ref.dtype)


def _mixin_lru_kernel(*refs, tiles_per_batch):
    kmax_ref, tail_sc, h_sc = refs[-3:]

    @pl.when(pl.program_id(0) % tiles_per_batch == 0)
    def _():
        kmax_ref[...] = jnp.zeros(kmax_ref.shape, _F32)
        tail_sc[...] = jnp.zeros(tail_sc.shape, _F32)
        h_sc[...] = jnp.zeros(h_sc.shape, _F32)

    _alternate(lambda n, rows: _mixin_lru_stages(n, rows, *refs), BIG_ROW_TILE)


def _mixin_lru(x, pos, inv, g_pre, w_in, g_q, w_uq, g_kv, w_k, w_vt, conv_w, conv_b, w_ax, b_ax,
               lam, batch, seq):
    n = x.shape[0]
    nsb = seq // BIG_ROW_TILE
    assert ROW_TILE == ATTN_TILE
    blocks = BIG_ROW_TILE // ATTN_TILE
    row = lambda w: pl.BlockSpec((BIG_ROW_TILE, w), lambda i: (i, 0))
    head = lambda w: pl.BlockSpec((None, MLA_HEADS, BIG_ROW_TILE, w),
                                  lambda i: (i // nsb, 0, i % nsb, 0))
    return pl.pallas_call(
        functools.partial(_mixin_lru_kernel, tiles_per_batch=nsb),
        grid=(n // BIG_ROW_TILE,),
        in_specs=[row(D_MODEL), row(1), _const_spec((1, LANES)), _const_spec((1, D_MODEL)),
                  _const_spec((D_MODEL, _IN_COLS_EXT)), _const_spec((1, Q_LORA_RANK)),
                  _const_spec((Q_LORA_RANK, MLA_HEADS * QK_PAD)),
                  _const_spec((1, KV_LORA_RANK)),
                  _const_spec((KV_LORA_RANK, MLA_HEADS * QK_NOPE_DIM)),
                  _const_spec((MLA_HEADS * V_HEAD_DIM, KV_LORA_RANK)),
                  _const_spec((CONV_WIDTH, LRU_WIDTH)), _const_spec((1, LRU_WIDTH)),
                  _const_spec((LRU_WIDTH, 2 * LRU_WIDTH)), _const_spec((1, 2 * LRU_WIDTH)),
                  _const_spec((1, LRU_WIDTH))],
        out_specs=[head(QK_PAD), head(QK_PAD),
                   pl.BlockSpec((None, MLA_HEADS, blocks, V_HEAD_DIM, ATTN_TILE),
                                lambda i: (i // nsb, 0, i % nsb, 0, 0)),
                   row(LRU_WIDTH),
                   pl.BlockSpec((None, SUBLANES, LANES), lambda i: (i // nsb, 0, 0))],
        out_shape=[jax.ShapeDtypeStruct((batch, MLA_HEADS, seq, QK_PAD), _BF16),
                   jax.ShapeDtypeStruct((batch, MLA_HEADS, seq, QK_PAD), _BF16),
                   jax.ShapeDtypeStruct((batch, MLA_HEADS, seq // ATTN_TILE, V_HEAD_DIM,
                                         ATTN_TILE), _BF16),
                   jax.ShapeDtypeStruct((n, LRU_WIDTH), _BF16),
                   jax.ShapeDtypeStruct((batch, SUBLANES, LANES), _F32)],
        scratch_shapes=[pltpu.VMEM((SUBLANES, LRU_WIDTH), _F32), pltpu.VMEM((1, LRU_WIDTH), _F32)],
        compiler_params=_params(("arbitrary",), 56),
        name="mixer_in_rglru",
    )(x, pos, inv, g_pre, w_in, g_q, w_uq, g_kv, w_k, w_vt, conv_w, conv_b, w_ax, b_ax, lam)


SHIFT_HEADROOM = 90.0
FIXED_SHIFT_WIDTH = 4
NORM_MARGIN = 1.02


def _attn_kernel(q_ref, k_ref, vt_ref, kmax_ref, o_ref, m_sc, l_sc, acc_sc):
    qi = pl.program_id(1)

    def scores(j, nb, hd):
        k = k_ref[hd, pl.ds(pl.multiple_of(j * ATTN_TILE, ATTN_TILE), nb * ATTN_TILE), :]
        return lax.dot_general(k, q_ref[hd], (((1,), (1,)), ((), ())),
                               preferred_element_type=_F32)

    def values_t(j, nb, hd):
        return jnp.concatenate([vt_ref[hd, j + r] for r in range(nb)], axis=1)

    def run_items(items, softmax_update):
        s_next = scores(*items[0])
        pending = None
        for n, item in enumerate(items):
            s = s_next
            if n + 1 < len(items):
                s_next = scores(*items[n + 1])
            if pending is not None:
                pending()
            pending = softmax_update(*item, s)
        pending()

    def all_heads(j, nb):
        return [(j, nb, hd) for hd in range(MLA_HEADS)]

    def diagonal(j, nb, hd, s):
        key = lax.broadcasted_iota(jnp.int32, s.shape, 0)
        qry = lax.broadcasted_iota(jnp.int32, s.shape, 1)
        s = jnp.where(key <= qry, s, NEG_INF)
        m = jnp.max(s, axis=0, keepdims=True)
        p = jnp.exp2(s - m)
        m_sc[hd] = m
        l_sc[hd] = jnp.sum(p, axis=0, keepdims=True)
        p = p.astype(_BF16)

        def values():
            acc_sc[hd] = jnp.dot(values_t(j, nb, hd), p, preferred_element_type=_F32)
        return values

    def fixed_shift(j, nb, hd, s):
        p = jnp.exp2(s - m_sc[hd])
        l_sc[hd] = l_sc[hd] + jnp.sum(p, axis=0, keepdims=True)
        p = p.astype(_BF16)

        def values():
            acc_sc[hd] = acc_sc[hd] + jnp.dot(values_t(j, nb, hd), p,
                                              preferred_element_type=_F32)
        return values

    def running_max(j, nb, hd, s):
        m_old = m_sc[hd]
        m_new = jnp.maximum(m_old, jnp.max(s, axis=0, keepdims=True))
        p = jnp.exp2(s - m_new)
        alpha = jnp.exp2(m_old - m_new)
        l_sc[hd] = alpha * l_sc[hd] + jnp.sum(p, axis=0, keepdims=True)
        m_sc[hd] = m_new
        p = p.astype(_BF16)

        def values():
            pv = jnp.dot(values_t(j, nb, hd), p, preferred_element_type=_F32)
            acc_sc[hd] = alpha * acc_sc[hd] + pv
        return values

    run_items(all_heads(qi, 1), diagonal)

    excess = jnp.zeros((1, ATTN_TILE), _F32)
    ones = jnp.ones((2 * SUBLANES, QK_PAD), _BF16)
    for hd in range(MLA_HEADS):
        q = q_ref[hd]
        q_sq = lax.dot_general(ones, q * q, (((1,), (1,)), ((), ())),
                               preferred_element_type=_F32)[0:1]
        k_sq = jnp.max(kmax_ref[hd:hd + 1, :], axis=1, keepdims=True)
        bound = jnp.sqrt(q_sq * k_sq) * NORM_MARGIN
        excess = jnp.maximum(excess, bound - m_sc[hd])
    fixed_ok = jnp.max(excess) <= SHIFT_HEADROOM

    def loop_with(update, width):
        def body(t, carry):
            run_items(all_heads(t * width, width), update)
            return carry
        lax.fori_loop(0, qi // width, body, 0)
        part = width // 2
        while part >= 1:
            @pl.when(qi % (2 * part) >= part)
            def _(part=part):
                run_items(all_heads(qi - qi % (2 * part), part), update)
            part //= 2

    @pl.when(fixed_ok)
    def _():
        loop_with(fixed_shift, FIXED_SHIFT_WIDTH)

    @pl.when(jnp.logical_not(fixed_ok))
    def _():
        loop_with(running_max, 1)

    for hd in range(MLA_HEADS):
        out = acc_sc[hd] / l_sc[hd]
        o_ref[:, hd * V_HEAD_DIM:(hd + 1) * V_HEAD_DIM] = out.T.astype(o_ref.dtype)


def _attention(q, k, vt, kmax, batch, seq):
    nq = seq // ATTN_TILE
    return pl.pallas_call(
        _attn_kernel,
        grid=(batch, nq),
        in_specs=[pl.BlockSpec((None, MLA_HEADS, ATTN_TILE, QK_PAD), lambda b, i: (b, 0, i, 0)),
                  pl.BlockSpec((None, MLA_HEADS, seq, QK_PAD), lambda b, i: (b, 0, 0, 0),
                               pipeline_mode=pl.Buffered(1)),
                  pl.BlockSpec((None, MLA_HEADS, nq, V_HEAD_DIM, ATTN_TILE),
                               lambda b, i: (b, 0, 0, 0, 0), pipeline_mode=pl.Buffered(1)),
                  pl.BlockSpec((None, SUBLANES, LANES), lambda b, i: (b, 0, 0))],
        out_specs=pl.BlockSpec((None, ATTN_TILE, MLA_WIDTH), lambda b, i: (b, i, 0)),
        out_shape=jax.ShapeDtypeStruct((batch, seq, MLA_WIDTH), _BF16),
        scratch_shapes=[pltpu.VMEM((MLA_HEADS, 1, ATTN_TILE), _F32),
                        pltpu.VMEM((MLA_HEADS, 1, ATTN_TILE), _F32),
                        pltpu.VMEM((MLA_HEADS, V_HEAD_DIM, ATTN_TILE), _F32)],
        compiler_params=_params(("parallel", "arbitrary"), 48),
        name="mla_attention",
    )(q, k, vt, kmax)


def _memkv_kernel(mem_ref, g_ref, w_ref, o_ref):
    m = _rmsnorm(mem_ref[...], g_ref[...]).astype(_BF16)
    o_ref[...] = jnp.dot(m, w_ref[...], preferred_element_type=_F32).astype(o_ref.dtype)


def _memkv(mem, g, w_kv):
    batch, mlen, _ = mem.shape
    return pl.pallas_call(
        _memkv_kernel,
        grid=(batch,),
        in_specs=[pl.BlockSpec((None, mlen, D_MODEL), lambda b: (b, 0, 0)),
                  _const_spec((1, D_MODEL)), _const_spec((D_MODEL, 2 * XA_WIDTH))],
        out_specs=pl.BlockSpec((None, mlen, 2 * XA_WIDTH), lambda b: (b, 0, 0)),
        out_shape=jax.ShapeDtypeStruct((batch, mlen, 2 * XA_WIDTH), _BF16),
        compiler_params=_params(("parallel",), 32),
        name="mem_kv",
    )(mem, g, w_kv)


def _mixout_ffn_stages(rows, x_ref, ymla_ref, ylru_ref, wout_ref, gmix_ref, gxa_ref, wq_ref,
                       kv_ref, wo_ref, gxo_ref, fgpre_ref, fwgu_ref, fwd_ref, fgpost_ref, o_ref):
    y = jnp.dot(ymla_ref[rows, :], wout_ref[0:MLA_WIDTH, :], preferred_element_type=_F32)
    y = y + jnp.dot(ylru_ref[rows, :], wout_ref[MLA_WIDTH:D_MODEL, :],
                    preferred_element_type=_F32)
    yield
    x = x_ref[rows, :] + _rmsnorm(y, gmix_ref[...])
    h = _rmsnorm(x, gxa_ref[...]).astype(_BF16)
    q = jnp.dot(h, wq_ref[...], preferred_element_type=_F32)
    q = (q * ((1.0 / math.sqrt(XA_HEAD_DIM)) * math.log2(math.e))).astype(_BF16)
    heads = []
    for hd in range(XA_HEADS):
        yield
        lo = hd * XA_HEAD_DIM
        kh = kv_ref[:, lo:lo + XA_HEAD_DIM]
        vh = kv_ref[:, XA_WIDTH + lo:XA_WIDTH + lo + XA_HEAD_DIM]
        s = lax.dot_general(q[:, lo:lo + XA_HEAD_DIM], kh, (((1,), (1,)), ((), ())),
                            preferred_element_type=_F32)
        e = jnp.exp2(s - jnp.max(s, axis=-1, keepdims=True))
        inv_sum = 1.0 / jnp.sum(e, axis=-1, keepdims=True)
        oh = jnp.dot(e.astype(_BF16), vh, preferred_element_type=_F32) * inv_sum
        heads.append(oh.astype(_BF16))
    yield
    o = jnp.concatenate(heads, axis=-1)
    y2 = jnp.dot(o, wo_ref[...], preferred_element_type=_F32)
    yield
    x = x + _rmsnorm(y2, gxo_ref[...])
    o_ref[rows, :] = yield from _ffn_stages(x, fgpre_ref, fwgu_ref, fwd_ref, fgpost_ref)


def _mixout_ffn_kernel(*refs):
    _alternate(lambda n, rows: _mixout_ffn_stages(rows, *refs), BIG_ROW_TILE)


def _mixout_ffn(x, y_mla, y_lru, w_out, g_mix, g_xa, w_q, kv, w_o, g_xo, ffn_args, seq):
    n = x.shape[0]
    nsb = seq // BIG_ROW_TILE
    mlen = kv.shape[1]
    row = lambda w: pl.BlockSpec((BIG_ROW_TILE, w), lambda i: (i, 0))
    return pl.pallas_call(
        _mixout_ffn_kernel,
        grid=(n // BIG_ROW_TILE,),
        in_specs=[row(D_MODEL), row(MLA_WIDTH), row(LRU_WIDTH),
                  _const_spec((D_MODEL, D_MODEL)), _const_spec((1, D_MODEL)),
                  _const_spec((1, D_MODEL)), _const_spec((D_MODEL, XA_WIDTH)),
                  pl.BlockSpec((None, mlen, 2 * XA_WIDTH), lambda i: (i // nsb, 0, 0)),
                  _const_spec((XA_WIDTH, D_MODEL)), _const_spec((1, D_MODEL)), *_ffn_specs()],
        out_specs=row(D_MODEL),
        out_shape=jax.ShapeDtypeStruct((n, D_MODEL), _F32),
        compiler_params=_params(("parallel",), 56),
        name="mixer_out_xattn_ffn",
    )(x, y_mla, y_lru, w_out, g_mix, g_xa, w_q, kv, w_o, g_xo, *ffn_args)


def _swap_halves(w):
    half = w.shape[-1] // 2
    return jnp.concatenate([w[..., half:], w[..., :half]], axis=-1)


def _prep_w_in(w_in):
    o1 = Q_LORA_RANK
    o2 = o1 + KV_LORA_RANK
    o3 = o2 + QK_ROPE_DIM
    o4 = o3 + LRU_WIDTH
    k_pe = w_in[:, o2:o3]
    return jnp.concatenate([w_in[:, :o1], w_in[:, o1:o2], w_in[:, o3:o4], w_in[:, o4:],
                            k_pe, _swap_halves(k_pe)], axis=-1).astype(_BF16)


def _prep_w_uq(w_uq):
    w = w_uq.reshape(Q_LORA_RANK, MLA_HEADS, QK_DIM)
    pe = w[..., QK_NOPE_DIM:]
    w = jnp.concatenate([w[..., :QK_NOPE_DIM], pe, _swap_halves(pe)], axis=-1)
    return w.reshape(Q_LORA_RANK, MLA_HEADS * QK_PAD).astype(_BF16)


def _prep_w_ukv(w_ukv):
    w = w_ukv.reshape(KV_LORA_RANK, MLA_HEADS, QK_NOPE_DIM + V_HEAD_DIM)
    w_k = w[..., :QK_NOPE_DIM].reshape(KV_LORA_RANK, MLA_HEADS * QK_NOPE_DIM)
    w_v = w[..., QK_NOPE_DIM:].reshape(KV_LORA_RANK, MLA_HEADS * V_HEAD_DIM)
    return w_k.astype(_BF16), w_v.T.astype(_BF16)


def _block_diag(w):
    nb, d, e = w.shape
    eye = jnp.eye(nb, dtype=w.dtype)
    return (eye[:, None, :, None] * w[:, :, None, :]).reshape(nb * d, nb * e)


def kernel(x, mem, positions, ffn1_pre_g, ffn1_w_gu, ffn1_w_down, ffn1_post_g, mix_pre_g, w_in, q_a_norm_g, w_uq, kv_a_norm_g, w_ukv, conv_w, conv_b, rg_w_a, rg_b_a, rg_w_x, rg_b_x, rg_lambda, w_out, mix_post_g, xa_pre_g, mem_norm_g, xa_w_q, xa_w_kv, xa_w_o, xa_post_g, ffn2_pre_g, ffn2_w_gu, ffn2_w_down, ffn2_post_g):
    batch, seq, _ = x.shape
    n = batch * seq
    depth = ffn1_pre_g.shape[0]
    bf = lambda w: w.astype(_BF16)
    vec = lambda g: g.reshape(1, -1)

    inv = ROPE_THETA ** (-jnp.arange(0, QK_ROPE_DIM, 2, dtype=_F32) / QK_ROPE_DIM)
    inv = jnp.tile(inv, 2 * LANES // QK_ROPE_DIM).reshape(1, LANES)
    pos = positions.reshape(n, 1)

    xf = x.reshape(n, D_MODEL)
    for l in range(depth):
        ffn1 = (vec(ffn1_pre_g[l]), bf(ffn1_w_gu[l]), bf(ffn1_w_down[l]), vec(ffn1_post_g[l]))
        ffn2 = (vec(ffn2_pre_g[l]), bf(ffn2_w_gu[l]), bf(ffn2_w_down[l]), vec(ffn2_post_g[l]))
        w_k, w_vt = _prep_w_ukv(w_ukv[l])
        w_ax = bf(jnp.concatenate([_block_diag(rg_w_a[l]), _block_diag(rg_w_x[l])], axis=-1))
        b_ax = jnp.concatenate([rg_b_a[l].reshape(1, -1), rg_b_x[l].reshape(1, -1)], axis=-1)
        xf = _ffn(xf, ffn1)
        q, k, vt, y_lru, kmax = _mixin_lru(
            xf, pos, inv, vec(mix_pre_g[l]), _prep_w_in(w_in[l]), vec(q_a_norm_g[l]),
            _prep_w_uq(w_uq[l]), vec(kv_a_norm_g[l]), w_k, w_vt, conv_w[l], vec(conv_b[l]),
            w_ax, b_ax, vec(rg_lambda[l]), batch, seq)
        y_mla = _attention(q, k, vt, kmax, batch, seq).reshape(n, MLA_WIDTH)

        kv = _memkv(mem, vec(mem_norm_g[l]), bf(xa_w_kv[l]))
        xf = _mixout_ffn(xf, y_mla, y_lru, bf(w_out[l]), vec(mix_post_g[l]), vec(xa_pre_g[l]),
                         bf(xa_w_q[l]), kv, bf(xa_w_o[l]), vec(xa_post_g[l]), ffn2, seq)
    return xf.reshape(batch, seq, D_MODEL)
```

```python
import functools
import math

import jax
import jax.numpy as jnp
from jax import lax
from jax.experimental import pallas as pl
from jax.experimental.pallas import tpu as pltpu

D_MODEL = 1024
MLA_HEADS = 4
QK_NOPE_DIM = 128
QK_ROPE_DIM = 64
QK_DIM = QK_NOPE_DIM + QK_ROPE_DIM
V_HEAD_DIM = 128
Q_LORA_RANK = 384
KV_LORA_RANK = 256
MLA_WIDTH = MLA_HEADS * V_HEAD_DIM
ROPE_THETA = 10000.0
LRU_WIDTH = D_MODEL - MLA_WIDTH
LRU_BLOCKS = 8
CONV_WIDTH = 4
LRU_C = 8.0
XA_HEADS = 4
XA_HEAD_DIM = 128
XA_WIDTH = XA_HEADS * XA_HEAD_DIM
D_FF = 2816
EPS = 1e-6
NEG_INF = -1e30

LANES = 128
SUBLANES = 8
MXU_DIM = 256
VMEM_BYTES_V7X = 64 * 1024 * 1024

QK_PAD = 2 * LANES

ROW_TILE = 512
BIG_ROW_TILE = 1024
ATTN_TILE = 512
FFN_CHUNKS = (768, 768, 768, 512)
assert sum(FFN_CHUNKS) == D_FF and all(c % MXU_DIM == 0 for c in FFN_CHUNKS)

_BF16 = jnp.bfloat16
_F32 = jnp.float32


def _rmsnorm(x, g):
    return x * lax.rsqrt(jnp.mean(x * x, axis=-1, keepdims=True) + EPS) * g


def _const_spec(shape):
    nd = len(shape)
    return pl.BlockSpec(shape, lambda *_: (0,) * nd, pipeline_mode=pl.Buffered(1))


def _params(semantics, vmem_mb):
    return pltpu.CompilerParams(dimension_semantics=semantics,
                                vmem_limit_bytes=vmem_mb * 1024 * 1024)


def _alternate(make_stages, tile_rows):
    subtiles = [make_stages(n, pl.ds(n * ROW_TILE, ROW_TILE)) for n in range(tile_rows // ROW_TILE)]
    while subtiles:
        for gen in list(subtiles):
            if next(gen, StopIteration) is StopIteration:
                subtiles.remove(gen)


def _ffn_stages(x, gpre_ref, wgu_ref, wd_ref, gpost_ref):
    h = _rmsnorm(x, gpre_ref[...]).astype(_BF16)
    acc = None
    off = 0
    for c in FFN_CHUNKS:
        g = jnp.dot(h, wgu_ref[:, off:off + c], preferred_element_type=_F32)
        u = jnp.dot(h, wgu_ref[:, D_FF + off:D_FF + off + c], preferred_element_type=_F32)
        a = (jax.nn.silu(g) * u).astype(_BF16)
        part = jnp.dot(a, wd_ref[off:off + c, :], preferred_element_type=_F32)
        acc = part if acc is None else acc + part
        off += c
        yield
    return x + 0.5 * _rmsnorm(acc, gpost_ref[...])


def _ffn_specs():
    return [_const_spec((1, D_MODEL)), _const_spec((D_MODEL, 2 * D_FF)),
            _const_spec((D_FF, D_MODEL)), _const_spec((1, D_MODEL))]


def _ffn_kernel(x_ref, gpre_ref, wgu_ref, wd_ref, gpost_ref, o_ref):
    def stages(n, rows):
        o_ref[rows, :] = yield from _ffn_stages(x_ref[rows, :], gpre_ref, wgu_ref, wd_ref,
                                                gpost_ref)
    _alternate(stages, BIG_ROW_TILE)


def _ffn(x, ffn_args):
    n = x.shape[0]
    row = pl.BlockSpec((BIG_ROW_TILE, D_MODEL), lambda i: (i, 0))
    return pl.pallas_call(
        _ffn_kernel,
        grid=(n // BIG_ROW_TILE,),
        in_specs=[row, *_ffn_specs()],
        out_specs=row,
        out_shape=jax.ShapeDtypeStruct((n, D_MODEL), _F32),
        compiler_params=_params(("parallel",), 56),
        name="ffn",
    )(x, *ffn_args)


def _sigmoid(x):
    return 0.5 * jnp.tanh(0.5 * x) + 0.5


def _linear_recurrence(a, b, h0):
    t, c = a.shape
    groups = t // SUBLANES
    a = a.reshape(groups, SUBLANES, c)
    b = b.reshape(groups, SUBLANES, c)
    sub = lax.broadcasted_iota(jnp.int32, a.shape, 1)
    d = 1
    while d < SUBLANES:
        keep = sub >= d
        b = a * jnp.where(keep, pltpu.roll(b, d, axis=1), 0.0) + b
        a = a * jnp.where(keep, pltpu.roll(a, d, axis=1), 1.0)
        d *= 2
    out = []
    for g in range(groups):
        hg = b[g] + a[g] * h0
        out.append(hg)
        h0 = hg[SUBLANES - 1:SUBLANES, :]
    return jnp.concatenate(out, axis=0)


def _rglru(u, gate, cw_ref, cb_ref, wax_ref, bax_ref, lam_ref, tail_sc, h_sc):
    tail = tail_sc[...]
    row8 = lax.broadcasted_iota(jnp.int32, tail.shape, 0)
    xc = cb_ref[...] + u * cw_ref[CONV_WIDTH - 1:CONV_WIDTH, :]
    for d in range(1, CONV_WIDTH):
        rolled = pltpu.roll(u, d, axis=0)
        head = jnp.where(row8 < d, pltpu.roll(tail, d, axis=0), rolled[:SUBLANES])
        shifted = jnp.concatenate([head, rolled[SUBLANES:]], axis=0)
        xc = xc + shifted * cw_ref[CONV_WIDTH - 1 - d:CONV_WIDTH - d, :]
    tail_sc[...] = u[u.shape[0] - SUBLANES:, :]

    pre = jnp.dot(xc.astype(_BF16), wax_ref[...], preferred_element_type=_F32) + bax_ref[...]
    r = _sigmoid(pre[:, :LRU_WIDTH])
    i = _sigmoid(pre[:, LRU_WIDTH:])
    neg_lam = -lam_ref[...]
    softplus = jnp.maximum(neg_lam, 0.0) + jnp.log1p(jnp.exp(-jnp.abs(neg_lam)))
    log_a = -LRU_C * r * softplus
    a = jnp.exp(log_a)
    b = jnp.sqrt(-jnp.tanh(log_a) * (a * a + 1.0)) * (i * xc)

    hseq = _linear_recurrence(a, b, h_sc[...])
    h_sc[...] = hseq[hseq.shape[0] - 1:, :]
    return hseq * jax.nn.gelu(gate)


_EXP2_SCALE = (1.0 / math.sqrt(QK_DIM)) * math.log2(math.e)

_C_Q = 0
_C_KV = _C_Q + Q_LORA_RANK
_C_U = _C_KV + KV_LORA_RANK
_C_GATE = _C_U + LRU_WIDTH
_C_KPE = _C_GATE + LRU_WIDTH
_IN_COLS_EXT = _C_KPE + 2 * QK_ROPE_DIM


def _rope_table(pos_ref, rows, inv):
    freqs = QK_ROPE_DIM // 2
    parts = LANES // freqs
    quarter = rows.size // parts
    lane = lax.broadcasted_iota(jnp.int32, (quarter, LANES), 1)
    pos = None
    for g in range(parts):
        pg = pos_ref[pl.ds(rows.start + g * quarter, quarter), :].astype(_F32)
        pos = pg if pos is None else jnp.where(lane < g * freqs, pos, pg)
    ang = pos * inv
    cos, sin = jnp.cos(ang), jnp.sin(ang)
    out = []
    for g in range(parts):
        def placed(x, t):
            shift = (freqs * (t - g)) % LANES
            return pltpu.roll(x, shift, axis=1) if shift else x
        out.append(jnp.where(lane < freqs, placed(cos, 0),
                             jnp.where(lane < 2 * freqs, placed(cos, 1),
                                       jnp.where(lane < 3 * freqs, -placed(sin, 2),
                                                 placed(sin, 3)))))
    return jnp.concatenate(out, axis=0)


def _mixin_lru_stages(n, rows, x_ref, pos_ref, inv_ref, gpre_ref, win_ref, gq_ref, wuq_ref,
                      gkv_ref, wk_ref, wvt_ref, cw_ref, cb_ref, wax_ref, bax_ref, lam_ref,
                      q_ref, k_ref, vt_ref, ylru_ref, kmax_ref, tail_sc, h_sc):
    h = _rmsnorm(x_ref[rows, :], gpre_ref[...]).astype(_BF16)
    z = jnp.dot(h, win_ref[...], preferred_element_type=_F32)
    yield

    table = _rope_table(pos_ref, rows, inv_ref[...])
    lane = lax.broadcasted_iota(jnp.int32, table.shape, 1)
    low = (lane < QK_ROPE_DIM).astype(_F32)

    def rope(pair):
        prod = pair * table
        return prod + pltpu.roll(prod, QK_ROPE_DIM, axis=1)

    k_rot = rope(z[:, _C_KPE:_IN_COLS_EXT])
    k_rot_sq = jnp.sum(k_rot * k_rot, axis=1, keepdims=True)
    k_rot = k_rot.astype(_BF16)

    c_q = _rmsnorm(z[:, _C_Q:_C_KV], gq_ref[...]).astype(_BF16)
    q_all = jnp.dot(c_q, wuq_ref[...], preferred_element_type=_F32) * _EXP2_SCALE
    c_kv = _rmsnorm(z[:, _C_KV:_C_U], gkv_ref[...]).astype(_BF16)
    k_nope = jnp.dot(c_kv, wk_ref[...], preferred_element_type=_F32)
    v_t = lax.dot_general(wvt_ref[...], c_kv, (((1,), (1,)), ((), ())),
                          preferred_element_type=_F32)
    yield
    for hd in range(MLA_HEADS):
        qh = q_all[:, hd * QK_PAD:(hd + 1) * QK_PAD]
        q_ref[hd, rows, 0:LANES] = qh[:, 0:LANES].astype(_BF16)
        q_ref[hd, rows, LANES:QK_PAD] = (rope(qh[:, LANES:QK_PAD]) * low).astype(_BF16)
        kh = k_nope[:, hd * LANES:(hd + 1) * LANES]
        k_ref[hd, rows, 0:LANES] = kh.astype(_BF16)
        k_ref[hd, rows, LANES:QK_PAD] = k_rot
        vt_ref[hd, n] = v_t[hd * V_HEAD_DIM:(hd + 1) * V_HEAD_DIM, :].astype(_BF16)
        k_sq = jnp.max(jnp.sum(kh * kh, axis=1, keepdims=True) + k_rot_sq, axis=0, keepdims=True)
        kmax_ref[hd:hd + 1, :] = jnp.maximum(kmax_ref[hd:hd + 1, :], k_sq)
    yield
    ylru_ref[rows, :] = _rglru(z[:, _C_U:_C_GATE], z[:, _C_GATE:_C_KPE], cw_ref, cb_ref, wax_ref,
                               bax_ref, lam_ref, tail_sc, h_sc).astype(ylru_ref.dtype)


def _mixin_lru_kernel(*refs, tiles_per_batch):
    kmax_ref, tail_sc, h_sc = refs[-3:]

    @pl.when(pl.program_id(0) % tiles_per_batch == 0)
    def _():
        kmax_ref[...] = jnp.zeros(kmax_ref.shape, _F32)
        tail_sc[...] = jnp.zeros(tail_sc.shape, _F32)
        h_sc[...] = jnp.zeros(h_sc.shape, _F32)

    _alternate(lambda n, rows: _mixin_lru_stages(n, rows, *refs), BIG_ROW_TILE)


def _mixin_lru(x, pos, inv, g_pre, w_in, g_q, w_uq, g_kv, w_k, w_vt, conv_w, conv_b, w_ax, b_ax,
               lam, batch, seq):
    n = x.shape[0]
    nsb = seq // BIG_ROW_TILE
    assert ROW_TILE == ATTN_TILE
    blocks = BIG_ROW_TILE // ATTN_TILE
    row = lambda w: pl.BlockSpec((BIG_ROW_TILE, w), lambda i: (i, 0))
    head = lambda w: pl.BlockSpec((None, MLA_HEADS, BIG_ROW_TILE, w),
                                  lambda i: (i // nsb, 0, i % nsb, 0))
    return pl.pallas_call(
        functools.partial(_mixin_lru_kernel, tiles_per_batch=nsb),
        grid=(n // BIG_ROW_TILE,),
        in_specs=[row(D_MODEL), row(1), _const_spec((1, LANES)), _const_spec((1, D_MODEL)),
                  _const_spec((D_MODEL, _IN_COLS_EXT)), _const_spec((1, Q_LORA_RANK)),
                  _const_spec((Q_LORA_RANK, MLA_HEADS * QK_PAD)),
                  _const_spec((1, KV_LORA_RANK)),
                  _const_spec((KV_LORA_RANK, MLA_HEADS * QK_NOPE_DIM)),
                  _const_spec((MLA_HEADS * V_HEAD_DIM, KV_LORA_RANK)),
                  _const_spec((CONV_WIDTH, LRU_WIDTH)), _const_spec((1, LRU_WIDTH)),
                  _const_spec((LRU_WIDTH, 2 * LRU_WIDTH)), _const_spec((1, 2 * LRU_WIDTH)),
                  _const_spec((1, LRU_WIDTH))],
        out_specs=[head(QK_PAD), head(QK_PAD),
                   pl.BlockSpec((None, MLA_HEADS, blocks, V_HEAD_DIM, ATTN_TILE),
                                lambda i: (i // nsb, 0, i % nsb, 0, 0)),
                   row(LRU_WIDTH),
                   pl.BlockSpec((None, SUBLANES, LANES), lambda i: (i // nsb, 0, 0))],
        out_shape=[jax.ShapeDtypeStruct((batch, MLA_HEADS, seq, QK_PAD), _BF16),
                   jax.ShapeDtypeStruct((batch, MLA_HEADS, seq, QK_PAD), _BF16),
                   jax.ShapeDtypeStruct((batch, MLA_HEADS, seq // ATTN_TILE, V_HEAD_DIM,
                                         ATTN_TILE), _BF16),
                   jax.ShapeDtypeStruct((n, LRU_WIDTH), _BF16),
                   jax.ShapeDtypeStruct((batch, SUBLANES, LANES), _F32)],
        scratch_shapes=[pltpu.VMEM((SUBLANES, LRU_WIDTH), _F32), pltpu.VMEM((1, LRU_WIDTH), _F32)],
        compiler_params=_params(("arbitrary",), 56),
        name="mixer_in_rglru",
    )(x, pos, inv, g_pre, w_in, g_q, w_uq, g_kv, w_k, w_vt, conv_w, conv_b, w_ax, b_ax, lam)


SHIFT_HEADROOM = 90.0
FIXED_SHIFT_WIDTH = 4
NORM_MARGIN = 1.02


def _attn_kernel(q_ref, kblk_ref, vtblk_ref, kmax_ref, o_ref, k_ref, vt_ref, m_sc, l_sc, acc_sc):
    qi = pl.program_id(1)
    k_ref[:, pl.ds(pl.multiple_of(qi * ATTN_TILE, ATTN_TILE), ATTN_TILE), :] = kblk_ref[...]
    vt_ref[:, qi] = vtblk_ref[:, 0]

    def scores(j, nb, hd):
        k = k_ref[hd, pl.ds(pl.multiple_of(j * ATTN_TILE, ATTN_TILE), nb * ATTN_TILE), :]
        return lax.dot_general(k, q_ref[hd], (((1,), (1,)), ((), ())),
                               preferred_element_type=_F32)

    def values_t(j, nb, hd):
        return jnp.concatenate([vt_ref[hd, j + r] for r in range(nb)], axis=1)

    def run_items(items, softmax_update):
        s_next = scores(*items[0])
        pending = None
        for n, item in enumerate(items):
            s = s_next
            if n + 1 < len(items):
                s_next = scores(*items[n + 1])
            if pending is not None:
                pending()
            pending = softmax_update(*item, s)
        pending()

    def all_heads(j, nb):
        return [(j, nb, hd) for hd in range(MLA_HEADS)]

    def diagonal(j, nb, hd, s):
        key = lax.broadcasted_iota(jnp.int32, s.shape, 0)
        qry = lax.broadcasted_iota(jnp.int32, s.shape, 1)
        s = jnp.where(key <= qry, s, NEG_INF)
        m = jnp.max(s, axis=0, keepdims=True)
        p = jnp.exp2(s - m)
        m_sc[hd] = m
        l_sc[hd] = jnp.sum(p, axis=0, keepdims=True)
        p = p.astype(_BF16)

        def values():
            acc_sc[hd] = jnp.dot(values_t(j, nb, hd), p, preferred_element_type=_F32)
        return values

    def fixed_shift(j, nb, hd, s):
        p = jnp.exp2(s - m_sc[hd])
        l_sc[hd] = l_sc[hd] + jnp.sum(p, axis=0, keepdims=True)
        p = p.astype(_BF16)

        def values():
            acc_sc[hd] = acc_sc[hd] + jnp.dot(values_t(j, nb, hd), p,
                                              preferred_element_type=_F32)
        return values

    def running_max(j, nb, hd, s):
        m_old = m_sc[hd]
        m_new = jnp.maximum(m_old, jnp.max(s, axis=0, keepdims=True))
        p = jnp.exp2(s - m_new)
        alpha = jnp.exp2(m_old - m_new)
        l_sc[hd] = alpha * l_sc[hd] + jnp.sum(p, axis=0, keepdims=True)
        m_sc[hd] = m_new
        p = p.astype(_BF16)

        def values():
            pv = jnp.dot(values_t(j, nb, hd), p, preferred_element_type=_F32)
            acc_sc[hd] = alpha * acc_sc[hd] + pv
        return values

    run_items(all_heads(qi, 1), diagonal)

    excess = jnp.zeros((1, ATTN_TILE), _F32)
    ones = jnp.ones((2 * SUBLANES, QK_PAD), _BF16)
    for hd in range(MLA_HEADS):
        q = q_ref[hd]
        q_sq = lax.dot_general(ones, q * q, (((1,), (1,)), ((), ())),
                               preferred_element_type=_F32)[0:1]
        k_sq = jnp.max(kmax_ref[hd:hd + 1, :], axis=1, keepdims=True)
        bound = jnp.sqrt(q_sq * k_sq) * NORM_MARGIN
        excess = jnp.maximum(excess, bound - m_sc[hd])
    fixed_ok = jnp.max(excess) <= SHIFT_HEADROOM

    def loop_with(update, width):
        def body(t, carry):
            run_items(all_heads(t * width, width), update)
            return carry
        lax.fori_loop(0, qi // width, body, 0)
        part = width // 2
        while part >= 1:
            @pl.when(qi % (2 * part) >= part)
            def _(part=part):
                run_items(all_heads(qi - qi % (2 * part), part), update)
            part //= 2

    @pl.when(fixed_ok)
    def _():
        loop_with(fixed_shift, FIXED_SHIFT_WIDTH)

    @pl.when(jnp.logical_not(fixed_ok))
    def _():
        loop_with(running_max, 1)

    for hd in range(MLA_HEADS):
        out = acc_sc[hd] / l_sc[hd]
        o_ref[:, hd * V_HEAD_DIM:(hd + 1) * V_HEAD_DIM] = out.T.astype(o_ref.dtype)


def _attention(q, k, vt, kmax, batch, seq):
    nq = seq // ATTN_TILE
    return pl.pallas_call(
        _attn_kernel,
        grid=(batch, nq),
        in_specs=[pl.BlockSpec((None, MLA_HEADS, ATTN_TILE, QK_PAD), lambda b, i: (b, 0, i, 0)),
                  pl.BlockSpec((None, MLA_HEADS, ATTN_TILE, QK_PAD), lambda b, i: (b, 0, i, 0)),
                  pl.BlockSpec((None, MLA_HEADS, 1, V_HEAD_DIM, ATTN_TILE),
                               lambda b, i: (b, 0, i, 0, 0)),
                  pl.BlockSpec((None, SUBLANES, LANES), lambda b, i: (b, 0, 0))],
        out_specs=pl.BlockSpec((None, ATTN_TILE, MLA_WIDTH), lambda b, i: (b, i, 0)),
        out_shape=jax.ShapeDtypeStruct((batch, seq, MLA_WIDTH), _BF16),
        scratch_shapes=[pltpu.VMEM((MLA_HEADS, seq, QK_PAD), _BF16),
                        pltpu.VMEM((MLA_HEADS, nq, V_HEAD_DIM, ATTN_TILE), _BF16),
                        pltpu.VMEM((MLA_HEADS, 1, ATTN_TILE), _F32),
                        pltpu.VMEM((MLA_HEADS, 1, ATTN_TILE), _F32),
                        pltpu.VMEM((MLA_HEADS, V_HEAD_DIM, ATTN_TILE), _F32)],
        compiler_params=_params(("parallel", "arbitrary"), 48),
        name="mla_attention",
    )(q, k, vt, kmax)


def _memkv_kernel(mem_ref, g_ref, w_ref, o_ref):
    m = _rmsnorm(mem_ref[...], g_ref[...]).astype(_BF16)
    o_ref[...] = jnp.dot(m, w_ref[...], preferred_element_type=_F32).astype(o_ref.dtype)


def _memkv(mem, g, w_kv):
    batch, mlen, _ = mem.shape
    return pl.pallas_call(
        _memkv_kernel,
        grid=(batch,),
        in_specs=[pl.BlockSpec((None, mlen, D_MODEL), lambda b: (b, 0, 0)),
                  _const_spec((1, D_MODEL)), _const_spec((D_MODEL, 2 * XA_WIDTH))],
        out_specs=pl.BlockSpec((None, mlen, 2 * XA_WIDTH), lambda b: (b, 0, 0)),
        out_shape=jax.ShapeDtypeStruct((batch, mlen, 2 * XA_WIDTH), _BF16),
        compiler_params=_params(("parallel",), 32),
        name="mem_kv",
    )(mem, g, w_kv)


def _mixout_ffn_stages(rows, x_ref, ymla_ref, ylru_ref, wout_ref, gmix_ref, gxa_ref, wq_ref,
                       kv_ref, wo_ref, gxo_ref, fgpre_ref, fwgu_ref, fwd_ref, fgpost_ref, o_ref):
    y = jnp.dot(ymla_ref[rows, :], wout_ref[0:MLA_WIDTH, :], preferred_element_type=_F32)
    y = y + jnp.dot(ylru_ref[rows, :], wout_ref[MLA_WIDTH:D_MODEL, :],
                    preferred_element_type=_F32)
    yield
    x = x_ref[rows, :] + _rmsnorm(y, gmix_ref[...])
    h = _rmsnorm(x, gxa_ref[...]).astype(_BF16)
    q = jnp.dot(h, wq_ref[...], preferred_element_type=_F32)
    q = (q * ((1.0 / math.sqrt(XA_HEAD_DIM)) * math.log2(math.e))).astype(_BF16)
    heads = []
    for hd in range(XA_HEADS):
        yield
        lo = hd * XA_HEAD_DIM
        kh = kv_ref[:, lo:lo + XA_HEAD_DIM]
        vh = kv_ref[:, XA_WIDTH + lo:XA_WIDTH + lo + XA_HEAD_DIM]
        s = lax.dot_general(q[:, lo:lo + XA_HEAD_DIM], kh, (((1,), (1,)), ((), ())),
                            preferred_element_type=_F32)
        e = jnp.exp2(s - jnp.max(s, axis=-1, keepdims=True))
        inv_sum = 1.0 / jnp.sum(e, axis=-1, keepdims=True)
        oh = jnp.dot(e.astype(_BF16), vh, preferred_element_type=_F32) * inv_sum
        heads.append(oh.astype(_BF16))
    yield
    o = jnp.concatenate(heads, axis=-1)
    y2 = jnp.dot(o, wo_ref[...], preferred_element_type=_F32)
    yield
    x = x + _rmsnorm(y2, gxo_ref[...])
    o_ref[rows, :] = yield from _ffn_stages(x, fgpre_ref, fwgu_ref, fwd_ref, fgpost_ref)


def _mixout_ffn_kernel(*refs):
    _alternate(lambda n, rows: _mixout_ffn_stages(rows, *refs), BIG_ROW_TILE)


def _mixout_ffn(x, y_mla, y_lru, w_out, g_mix, g_xa, w_q, kv, w_o, g_xo, ffn_args, seq):
    n = x.shape[0]
    nsb = seq // BIG_ROW_TILE
    mlen = kv.shape[1]
    row = lambda w: pl.BlockSpec((BIG_ROW_TILE, w), lambda i: (i, 0))
    return pl.pallas_call(
        _mixout_ffn_kernel,
        grid=(n // BIG_ROW_TILE,),
        in_specs=[row(D_MODEL), row(MLA_WIDTH), row(LRU_WIDTH),
                  _const_spec((D_MODEL, D_MODEL)), _const_spec((1, D_MODEL)),
                  _const_spec((1, D_MODEL)), _const_spec((D_MODEL, XA_WIDTH)),
                  pl.BlockSpec((None, mlen, 2 * XA_WIDTH), lambda i: (i // nsb, 0, 0)),
                  _const_spec((XA_WIDTH, D_MODEL)), _const_spec((1, D_MODEL)), *_ffn_specs()],
        out_specs=row(D_MODEL),
        out_shape=jax.ShapeDtypeStruct((n, D_MODEL), _F32),
        compiler_params=_params(("parallel",), 56),
        name="mixer_out_xattn_ffn",
    )(x, y_mla, y_lru, w_out, g_mix, g_xa, w_q, kv, w_o, g_xo, *ffn_args)


def _swap_halves(w):
    half = w.shape[-1] // 2
    return jnp.concatenate([w[..., half:], w[..., :half]], axis=-1)


def _prep_w_in(w_in):
    o1 = Q_LORA_RANK
    o2 = o1 + KV_LORA_RANK
    o3 = o2 + QK_ROPE_DIM
    o4 = o3 + LRU_WIDTH
    k_pe = w_in[:, o2:o3]
    return jnp.concatenate([w_in[:, :o1], w_in[:, o1:o2], w_in[:, o3:o4], w_in[:, o4:],
                            k_pe, _swap_halves(k_pe)], axis=-1).astype(_BF16)


def _prep_w_uq(w_uq):
    w = w_uq.reshape(Q_LORA_RANK, MLA_HEADS, QK_DIM)
    pe = w[..., QK_NOPE_DIM:]
    w = jnp.concatenate([w[..., :QK_NOPE_DIM], pe, _swap_halves(pe)], axis=-1)
    return w.reshape(Q_LORA_RANK, MLA_HEADS * QK_PAD).astype(_BF16)


def _prep_w_ukv(w_ukv):
    w = w_ukv.reshape(KV_LORA_RANK, MLA_HEADS, QK_NOPE_DIM + V_HEAD_DIM)
    w_k = w[..., :QK_NOPE_DIM].reshape(KV_LORA_RANK, MLA_HEADS * QK_NOPE_DIM)
    w_v = w[..., QK_NOPE_DIM:].reshape(KV_LORA_RANK, MLA_HEADS * V_HEAD_DIM)
    return w_k.astype(_BF16), w_v.T.astype(_BF16)


def _block_diag(w):
    nb, d, e = w.shape
    eye = jnp.eye(nb, dtype=w.dtype)
    return (eye[:, None, :, None] * w[:, :, None, :]).reshape(nb * d, nb * e)


def kernel(x, mem, positions, ffn1_pre_g, ffn1_w_gu, ffn1_w_down, ffn1_post_g, mix_pre_g, w_in, q_a_norm_g, w_uq, kv_a_norm_g, w_ukv, conv_w, conv_b, rg_w_a, rg_b_a, rg_w_x, rg_b_x, rg_lambda, w_out, mix_post_g, xa_pre_g, mem_norm_g, xa_w_q, xa_w_kv, xa_w_o, xa_post_g, ffn2_pre_g, ffn2_w_gu, ffn2_w_down, ffn2_post_g):
    batch, seq, _ = x.shape
    n = batch * seq
    depth = ffn1_pre_g.shape[0]
    bf = lambda w: w.astype(_BF16)
    vec = lambda g: g.reshape(1, -1)

    inv = ROPE_THETA ** (-jnp.arange(0, QK_ROPE_DIM, 2, dtype=_F32) / QK_ROPE_DIM)
    inv = jnp.tile(inv, 2 * LANES // QK_ROPE_DIM).reshape(1, LANES)
    pos = positions.reshape(n, 1)

    xf = x.reshape(n, D_MODEL)
    for l in range(depth):
        ffn1 = (vec(ffn1_pre_g[l]), bf(ffn1_w_gu[l]), bf(ffn1_w_down[l]), vec(ffn1_post_g[l]))
        ffn2 = (vec(ffn2_pre_g[l]), bf(ffn2_w_gu[l]), bf(ffn2_w_down[l]), vec(ffn2_post_g[l]))
        w_k, w_vt = _prep_w_ukv(w_ukv[l])
        w_ax = bf(jnp.concatenate([_block_diag(rg_w_a[l]), _block_diag(rg_w_x[l])], axis=-1))
        b_ax = jnp.concatenate([rg_b_a[l].reshape(1, -1), rg_b_x[l].reshape(1, -1)], axis=-1)
        xf = _ffn(xf, ffn1)
        q, k, vt, y_lru, kmax = _mixin_lru(
            xf, pos, inv, vec(mix_pre_g[l]), _prep_w_in(w_in[l]), vec(q_a_norm_g[l]),
            _prep_w_uq(w_uq[l]), vec(kv_a_norm_g[l]), w_k, w_vt, conv_w[l], vec(conv_b[l]),
            w_ax, b_ax, vec(rg_lambda[l]), batch, seq)
        y_mla = _attention(q, k, vt, kmax, batch, seq).reshape(n, MLA_WIDTH)

        kv = _memkv(mem, vec(mem_norm_g[l]), bf(xa_w_kv[l]))
        xf = _mixout_ffn(xf, y_mla, y_lru, bf(w_out[l]), vec(mix_post_g[l]), vec(xa_pre_g[l]),
                         bf(xa_w_q[l]), kv, bf(xa_w_o[l]), vec(xa_post_g[l]), ffn2, seq)
    return xf.reshape(batch, seq, D_MODEL)
```

```python
import functools
import math

import jax
import jax.numpy as jnp
from jax import lax
from jax.experimental import pallas as pl
from jax.experimental.pallas import tpu as pltpu

D_MODEL = 1024
MLA_HEADS = 4
QK_NOPE_DIM = 128
QK_ROPE_DIM = 64
QK_DIM = QK_NOPE_DIM + QK_ROPE_DIM
V_HEAD_DIM = 128
Q_LORA_RANK = 384
KV_LORA_RANK = 256
MLA_WIDTH = MLA_HEADS * V_HEAD_DIM
ROPE_THETA = 10000.0
LRU_WIDTH = D_MODEL - MLA_WIDTH
LRU_BLOCKS = 8
CONV_WIDTH = 4
LRU_C = 8.0
XA_HEADS = 4
XA_HEAD_DIM = 128
XA_WIDTH = XA_HEADS * XA_HEAD_DIM
D_FF = 2816
EPS = 1e-6
NEG_INF = -1e30

LANES = 128
SUBLANES = 8
MXU_DIM = 256
VMEM_BYTES_V7X = 64 * 1024 * 1024

QK_PAD = 2 * LANES

ROW_TILE = 512
BIG_ROW_TILE = 1024
ATTN_TILE = 512
FFN_CHUNKS = (768, 768, 768, 512)
assert sum(FFN_CHUNKS) == D_FF and all(c % MXU_DIM == 0 for c in FFN_CHUNKS)

_BF16 = jnp.bfloat16
_F32 = jnp.float32


def _rmsnorm(x, g):
    return x * lax.rsqrt(jnp.mean(x * x, axis=-1, keepdims=True) + EPS) * g


def _const_spec(shape):
    nd = len(shape)
    return pl.BlockSpec(shape, lambda *_: (0,) * nd, pipeline_mode=pl.Buffered(1))


def _params(semantics, vmem_mb):
    return pltpu.CompilerParams(dimension_semantics=semantics,
                                vmem_limit_bytes=vmem_mb * 1024 * 1024)


def _alternate(make_stages, tile_rows):
    subtiles = [make_stages(n, pl.ds(n * ROW_TILE, ROW_TILE)) for n in range(tile_rows // ROW_TILE)]
    while subtiles:
        for gen in list(subtiles):
            if next(gen, StopIteration) is StopIteration:
                subtiles.remove(gen)


def _ffn_stages(x, gpre_ref, wgu_ref, wd_ref, gpost_ref):
    h = _rmsnorm(x, gpre_ref[...]).astype(_BF16)
    acc = None
    off = 0
    for c in FFN_CHUNKS:
        g = jnp.dot(h, wgu_ref[:, off:off + c], preferred_element_type=_F32)
        u = jnp.dot(h, wgu_ref[:, D_FF + off:D_FF + off + c], preferred_element_type=_F32)
        a = (g * _sigmoid(g) * u).astype(_BF16)
        part = jnp.dot(a, wd_ref[off:off + c, :], preferred_element_type=_F32)
        acc = part if acc is None else acc + part
        off += c
        yield
    return x + 0.5 * _rmsnorm(acc, gpost_ref[...])


WEIGHT_STAGE_ROWS = {2 * D_FF: 32, D_MODEL: 176}


def _ffn_weight_scratch():
    return [pltpu.VMEM((D_MODEL, 2 * D_FF), _BF16), pltpu.VMEM((D_FF, D_MODEL), _BF16),
            pltpu.VMEM((2, WEIGHT_STAGE_ROWS[2 * D_FF], 2 * D_FF), _F32),
            pltpu.VMEM((2, WEIGHT_STAGE_ROWS[D_MODEL], D_MODEL), _F32),
            pltpu.SemaphoreType.DMA((2,)), pltpu.SemaphoreType.DMA((2,))]


def _stage_weight_bf16(src_hbm, dst_sc, stage_sc, sem):
    rows = stage_sc.shape[1]
    chunks = src_hbm.shape[0] // rows
    assert chunks * rows == src_hbm.shape[0] and rows % (2 * SUBLANES) == 0

    def copy(c):
        return pltpu.make_async_copy(src_hbm.at[pl.ds(c * rows, rows), :], stage_sc.at[c % 2],
                                     sem.at[c % 2])

    copy(0).start()
    for c in range(chunks):
        if c + 1 < chunks:
            copy(c + 1).start()
        copy(c).wait()
        dst_sc[c * rows:(c + 1) * rows, :] = stage_sc[c % 2].astype(_BF16)


def _ffn_kernel(x_ref, gpre_ref, wgu_hbm, wd_hbm, gpost_ref, o_ref, wgu_sc, wd_sc, stage_gu,
                stage_d, sem_gu, sem_d):
    @pl.when(pl.program_id(0) == 0)
    def _():
        _stage_weight_bf16(wgu_hbm, wgu_sc, stage_gu, sem_gu)
        _stage_weight_bf16(wd_hbm, wd_sc, stage_d, sem_d)

    def stages(n, rows):
        o_ref[rows, :] = yield from _ffn_stages(x_ref[rows, :], gpre_ref, wgu_sc, wd_sc,
                                                gpost_ref)
    _alternate(stages, BIG_ROW_TILE)


def _ffn(x, g_pre, w_gu, w_down, g_post):
    n = x.shape[0]
    row = pl.BlockSpec((BIG_ROW_TILE, D_MODEL), lambda i: (i, 0))
    hbm = pl.BlockSpec(memory_space=pl.ANY)
    return pl.pallas_call(
        _ffn_kernel,
        grid=(n // BIG_ROW_TILE,),
        in_specs=[row, _const_spec((1, D_MODEL)), hbm, hbm, _const_spec((1, D_MODEL))],
        out_specs=row,
        out_shape=jax.ShapeDtypeStruct((n, D_MODEL), _F32),
        scratch_shapes=_ffn_weight_scratch(),
        compiler_params=_params(("arbitrary",), 56),
        name="ffn",
    )(x, g_pre, w_gu, w_down, g_post)


def _sigmoid(x):
    return 0.5 * jnp.tanh(0.5 * x) + 0.5


def _linear_recurrence(a, b, h0):
    t, c = a.shape
    groups = t // SUBLANES
    a = a.reshape(groups, SUBLANES, c)
    b = b.reshape(groups, SUBLANES, c)
    sub = lax.broadcasted_iota(jnp.int32, a.shape, 1)
    d = 1
    while d < SUBLANES:
        keep = sub >= d
        b = a * jnp.where(keep, pltpu.roll(b, d, axis=1), 0.0) + b
        a = a * jnp.where(keep, pltpu.roll(a, d, axis=1), 1.0)
        d *= 2
    out = []
    for g in range(groups):
        hg = b[g] + a[g] * h0
        out.append(hg)
        h0 = hg[SUBLANES - 1:SUBLANES, :]
    return jnp.concatenate(out, axis=0)


def _rglru(u, gate, cw_ref, cb_ref, wax_ref, bax_ref, lam_ref, tail_sc, h_sc):
    tail = tail_sc[...]
    row8 = lax.broadcasted_iota(jnp.int32, tail.shape, 0)
    xc = cb_ref[...] + u * cw_ref[CONV_WIDTH - 1:CONV_WIDTH, :]
    for d in range(1, CONV_WIDTH):
        rolled = pltpu.roll(u, d, axis=0)
        head = jnp.where(row8 < d, pltpu.roll(tail, d, axis=0), rolled[:SUBLANES])
        shifted = jnp.concatenate([head, rolled[SUBLANES:]], axis=0)
        xc = xc + shifted * cw_ref[CONV_WIDTH - 1 - d:CONV_WIDTH - d, :]
    tail_sc[...] = u[u.shape[0] - SUBLANES:, :]

    pre = jnp.dot(xc.astype(_BF16), wax_ref[...], preferred_element_type=_F32) + bax_ref[...]
    r = _sigmoid(pre[:, :LRU_WIDTH])
    i = _sigmoid(pre[:, LRU_WIDTH:])
    neg_lam = -lam_ref[...]
    softplus = jnp.maximum(neg_lam, 0.0) + jnp.log1p(jnp.exp(-jnp.abs(neg_lam)))
    log_a = -LRU_C * r * softplus
    a = jnp.exp(log_a)
    b = jnp.sqrt(-jnp.tanh(log_a) * (a * a + 1.0)) * (i * xc)

    hseq = _linear_recurrence(a, b, h_sc[...])
    h_sc[...] = hseq[hseq.shape[0] - 1:, :]
    return hseq * jax.nn.gelu(gate)


_EXP2_SCALE = (1.0 / math.sqrt(QK_DIM)) * math.log2(math.e)

_C_Q = 0
_C_KV = _C_Q + Q_LORA_RANK
_C_U = _C_KV + KV_LORA_RANK
_C_GATE = _C_U + LRU_WIDTH
_C_KPE = _C_GATE + LRU_WIDTH
_IN_COLS_EXT = _C_KPE + 2 * QK_ROPE_DIM


def _rope_table(pos_ref, rows, inv):
    freqs = QK_ROPE_DIM // 2
    parts = LANES // freqs
    quarter = rows.size // parts
    lane = lax.broadcasted_iota(jnp.int32, (quarter, LANES), 1)
    pos = None
    for g in range(parts):
        pg = pos_ref[pl.ds(rows.start + g * quarter, quarter), :].astype(_F32)
        pos = pg if pos is None else jnp.where(lane < g * freqs, pos, pg)
    ang = pos * inv
    cos, sin = jnp.cos(ang), jnp.sin(ang)
    out = []
    for g in range(parts):
        def placed(x, t):
            shift = (freqs * (t - g)) % LANES
            return pltpu.roll(x, shift, axis=1) if shift else x
        out.append(jnp.where(lane < freqs, placed(cos, 0),
                             jnp.where(lane < 2 * freqs, placed(cos, 1),
                                       jnp.where(lane < 3 * freqs, -placed(sin, 2),
                                                 placed(sin, 3)))))
    return jnp.concatenate(out, axis=0)


def _mixin_lru_stages(n, rows, x_ref, pos_ref, inv_ref, gpre_ref, win_ref, gq_ref, wuq_ref,
                      gkv_ref, wk_ref, wvt_ref, cw_ref, cb_ref, wax_ref, bax_ref, lam_ref,
                      q_ref, k_ref, vt_ref, ylru_ref, kmax_ref, tail_sc, h_sc):
    h = _rmsnorm(x_ref[rows, :], gpre_ref[...]).astype(_BF16)
    z = jnp.dot(h, win_ref[...], preferred_element_type=_F32)
    yield

    table = _rope_table(pos_ref, rows, inv_ref[...])
    lane = lax.broadcasted_iota(jnp.int32, table.shape, 1)
    low = (lane < QK_ROPE_DIM).astype(_F32)

    def rope(pair):
        prod = pair * table
        return prod + pltpu.roll(prod, QK_ROPE_DIM, axis=1)

    k_rot = rope(z[:, _C_KPE:_IN_COLS_EXT])
    k_rot_sq = jnp.sum(k_rot * k_rot, axis=1, keepdims=True)
    k_rot = k_rot.astype(_BF16)

    c_q = _rmsnorm(z[:, _C_Q:_C_KV], gq_ref[...]).astype(_BF16)
    q_all = jnp.dot(c_q, wuq_ref[...], preferred_element_type=_F32) * _EXP2_SCALE
    c_kv = _rmsnorm(z[:, _C_KV:_C_U], gkv_ref[...]).astype(_BF16)
    k_nope = jnp.dot(c_kv, wk_ref[...], preferred_element_type=_F32)
    v_t = lax.dot_general(wvt_ref[...], c_kv, (((1,), (1,)), ((), ())),
                          preferred_element_type=_F32)
    yield
    for hd in range(MLA_HEADS):
        qh = q_all[:, hd * QK_PAD:(hd + 1) * QK_PAD]
        q_ref[hd, rows, 0:LANES] = qh[:, 0:LANES].astype(_BF16)
        q_ref[hd, rows, LANES:QK_PAD] = (rope(qh[:, LANES:QK_PAD]) * low).astype(_BF16)
        kh = k_nope[:, hd * LANES:(hd + 1) * LANES]
        k_ref[hd, rows, 0:LANES] = kh.astype(_BF16)
        k_ref[hd, rows, LANES:QK_PAD] = k_rot
        vt_ref[hd, n] = v_t[hd * V_HEAD_DIM:(hd + 1) * V_HEAD_DIM, :].astype(_BF16)
        k_sq = jnp.max(jnp.sum(kh * kh, axis=1, keepdims=True) + k_rot_sq, axis=0, keepdims=True)
        kmax_ref[hd:hd + 1, :] = jnp.maximum(kmax_ref[hd:hd + 1, :], k_sq)
    yield
    ylru_ref[rows, :] = _rglru(z[:, _C_U:_C_GATE], z[:, _C_GATE:_C_KPE], cw_ref, cb_ref, wax_ref,
                               bax_ref, lam_ref, tail_sc, h_sc).astype(ylru_ref.dtype)


def _mixin_lru_kernel(*refs, tiles_per_batch):
    kmax_ref, tail_sc, h_sc = refs[-3:]

    @pl.when(pl.program_id(0) % tiles_per_batch == 0)
    def _():
        kmax_ref[...] = jnp.zeros(kmax_ref.shape, _F32)
        tail_sc[...] = jnp.zeros(tail_sc.shape, _F32)
        h_sc[...] = jnp.zeros(h_sc.shape, _F32)

    _alternate(lambda n, rows: _mixin_lru_stages(n, rows, *refs), BIG_ROW_TILE)


def _mixin_lru(x, pos, inv, g_pre, w_in, g_q, w_uq, g_kv, w_k, w_vt, conv_w, conv_b, w_ax, b_ax,
               lam, batch, seq):
    n = x.shape[0]
    nsb = seq // BIG_ROW_TILE
    assert ROW_TILE == ATTN_TILE
    blocks = BIG_ROW_TILE // ATTN_TILE
    row = lambda w: pl.BlockSpec((BIG_ROW_TILE, w), lambda i: (i, 0))
    head = lambda w: pl.BlockSpec((None, MLA_HEADS, BIG_ROW_TILE, w),
                                  lambda i: (i // nsb, 0, i % nsb, 0))
    return pl.pallas_call(
        functools.partial(_mixin_lru_kernel, tiles_per_batch=nsb),
        grid=(n // BIG_ROW_TILE,),
        in_specs=[row(D_MODEL), row(1), _const_spec((1, LANES)), _const_spec((1, D_MODEL)),
                  _const_spec((D_MODEL, _IN_COLS_EXT)), _const_spec((1, Q_LORA_RANK)),
                  _const_spec((Q_LORA_RANK, MLA_HEADS * QK_PAD)),
                  _const_spec((1, KV_LORA_RANK)),
                  _const_spec((KV_LORA_RANK, MLA_HEADS * QK_NOPE_DIM)),
                  _const_spec((MLA_HEADS * V_HEAD_DIM, KV_LORA_RANK)),
                  _const_spec((CONV_WIDTH, LRU_WIDTH)), _const_spec((1, LRU_WIDTH)),
                  _const_spec((LRU_WIDTH, 2 * LRU_WIDTH)), _const_spec((1, 2 * LRU_WIDTH)),
                  _const_spec((1, LRU_WIDTH))],
        out_specs=[head(QK_PAD), head(QK_PAD),
                   pl.BlockSpec((None, MLA_HEADS, blocks, V_HEAD_DIM, ATTN_TILE),
                                lambda i: (i // nsb, 0, i % nsb, 0, 0)),
                   row(LRU_WIDTH),
                   pl.BlockSpec((None, SUBLANES, LANES), lambda i: (i // nsb, 0, 0))],
        out_shape=[jax.ShapeDtypeStruct((batch, MLA_HEADS, seq, QK_PAD), _BF16),
                   jax.ShapeDtypeStruct((batch, MLA_HEADS, seq, QK_PAD), _BF16),
                   jax.ShapeDtypeStruct((batch, MLA_HEADS, seq // ATTN_TILE, V_HEAD_DIM,
                                         ATTN_TILE), _BF16),
                   jax.ShapeDtypeStruct((n, LRU_WIDTH), _BF16),
                   jax.ShapeDtypeStruct((batch, SUBLANES, LANES), _F32)],
        scratch_shapes=[pltpu.VMEM((SUBLANES, LRU_WIDTH), _F32), pltpu.VMEM((1, LRU_WIDTH), _F32)],
        compiler_params=_params(("arbitrary",), 56),
        name="mixer_in_rglru",
    )(x, pos, inv, g_pre, w_in, g_q, w_uq, g_kv, w_k, w_vt, conv_w, conv_b, w_ax, b_ax, lam)


SHIFT_HEADROOM = 90.0
FIXED_SHIFT_WIDTH = 4
NORM_MARGIN = 1.02


def _attn_kernel(q_ref, kblk_ref, vtblk_ref, kmax_ref, o_ref, k_ref, vt_ref, m_sc, l_sc, acc_sc):
    qi = pl.program_id(1)
    k_ref[:, pl.ds(pl.multiple_of(qi * ATTN_TILE, ATTN_TILE), ATTN_TILE), :] = kblk_ref[...]
    vt_ref[:, qi] = vtblk_ref[:, 0]

    def scores(j, nb, hd):
        k = k_ref[hd, pl.ds(pl.multiple_of(j * ATTN_TILE, ATTN_TILE), nb * ATTN_TILE), :]
        return lax.dot_general(k, q_ref[hd], (((1,), (1,)), ((), ())),
                               preferred_element_type=_F32)

    def values_t(j, nb, hd):
        return jnp.concatenate([vt_ref[hd, j + r] for r in range(nb)], axis=1)

    def run_items(items, softmax_update):
        s_next = scores(*items[0])
        pending = None
        for n, item in enumerate(items):
            s = s_next
            if n + 1 < len(items):
                s_next = scores(*items[n + 1])
            if pending is not None:
                pending()
            pending = softmax_update(*item, s)
        pending()

    def all_heads(j, nb):
        return [(j, nb, hd) for hd in range(MLA_HEADS)]

    def diagonal(j, nb, hd, s):
        key = lax.broadcasted_iota(jnp.int32, s.shape, 0)
        qry = lax.broadcasted_iota(jnp.int32, s.shape, 1)
        s = jnp.where(key <= qry, s, NEG_INF)
        m = jnp.max(s, axis=0, keepdims=True)
        p = jnp.exp2(s - m)
        m_sc[hd] = m
        l_sc[hd] = jnp.sum(p, axis=0, keepdims=True)
        p = p.astype(_BF16)

        def values():
            acc_sc[hd] = jnp.dot(values_t(j, nb, hd), p, preferred_element_type=_F32)
        return values

    def fixed_shift(j, nb, hd, s):
        p = jnp.exp2(s - m_sc[hd])
        l_sc[hd] = l_sc[hd] + jnp.sum(p, axis=0, keepdims=True)
        p = p.astype(_BF16)

        def values():
            acc_sc[hd] = acc_sc[hd] + jnp.dot(values_t(j, nb, hd), p,
                                              preferred_element_type=_F32)
        return values

    def running_max(j, nb, hd, s):
        m_old = m_sc[hd]
        m_new = jnp.maximum(m_old, jnp.max(s, axis=0, keepdims=True))
        p = jnp.exp2(s - m_new)
        alpha = jnp.exp2(m_old - m_new)
        l_sc[hd] = alpha * l_sc[hd] + jnp.sum(p, axis=0, keepdims=True)
        m_sc[hd] = m_new
        p = p.astype(_BF16)

        def values():
            pv = jnp.dot(values_t(j, nb, hd), p, preferred_element_type=_F32)
            acc_sc[hd] = alpha * acc_sc[hd] + pv
        return values

    run_items(all_heads(qi, 1), diagonal)

    excess = jnp.zeros((1, ATTN_TILE), _F32)
    ones = jnp.ones((2 * SUBLANES, QK_PAD), _BF16)
    for hd in range(MLA_HEADS):
        q = q_ref[hd]
        q_sq = lax.dot_general(ones, q * q, (((1,), (1,)), ((), ())),
                               preferred_element_type=_F32)[0:1]
        k_sq = jnp.max(kmax_ref[hd:hd + 1, :], axis=1, keepdims=True)
        bound = jnp.sqrt(q_sq * k_sq) * NORM_MARGIN
        excess = jnp.maximum(excess, bound - m_sc[hd])
    fixed_ok = jnp.max(excess) <= SHIFT_HEADROOM

    def loop_with(update, width):
        def body(t, carry):
            run_items(all_heads(t * width, width), update)
            return carry
        lax.fori_loop(0, qi // width, body, 0)
        part = width // 2
        while part >= 1:
            @pl.when(qi % (2 * part) >= part)
            def _(part=part):
                run_items(all_heads(qi - qi % (2 * part), part), update)
            part //= 2

    @pl.when(fixed_ok)
    def _():
        loop_with(fixed_shift, FIXED_SHIFT_WIDTH)

    @pl.when(jnp.logical_not(fixed_ok))
    def _():
        loop_with(running_max, 1)

    for hd in range(MLA_HEADS):
        out = acc_sc[hd] / l_sc[hd]
        o_ref[:, hd * V_HEAD_DIM:(hd + 1) * V_HEAD_DIM] = out.T.astype(o_ref.dtype)


def _attention(q, k, vt, kmax, batch, seq):
    nq = seq // ATTN_TILE
    return pl.pallas_call(
        _attn_kernel,
        grid=(batch, nq),
        in_specs=[pl.BlockSpec((None, MLA_HEADS, ATTN_TILE, QK_PAD), lambda b, i: (b, 0, i, 0)),
                  pl.BlockSpec((None, MLA_HEADS, ATTN_TILE, QK_PAD), lambda b, i: (b, 0, i, 0)),
                  pl.BlockSpec((None, MLA_HEADS, 1, V_HEAD_DIM, ATTN_TILE),
                               lambda b, i: (b, 0, i, 0, 0)),
                  pl.BlockSpec((None, SUBLANES, LANES), lambda b, i: (b, 0, 0))],
        out_specs=pl.BlockSpec((None, ATTN_TILE, MLA_WIDTH), lambda b, i: (b, i, 0)),
        out_shape=jax.ShapeDtypeStruct((batch, seq, MLA_WIDTH), _BF16),
        scratch_shapes=[pltpu.VMEM((MLA_HEADS, seq, QK_PAD), _BF16),
                        pltpu.VMEM((MLA_HEADS, nq, V_HEAD_DIM, ATTN_TILE), _BF16),
                        pltpu.VMEM((MLA_HEADS, 1, ATTN_TILE), _F32),
                        pltpu.VMEM((MLA_HEADS, 1, ATTN_TILE), _F32),
                        pltpu.VMEM((MLA_HEADS, V_HEAD_DIM, ATTN_TILE), _F32)],
        compiler_params=_params(("parallel", "arbitrary"), 48),
        name="mla_attention",
    )(q, k, vt, kmax)


def _memkv_kernel(mem_ref, g_ref, w_ref, o_ref):
    m = _rmsnorm(mem_ref[...], g_ref[...]).astype(_BF16)
    o_ref[...] = jnp.dot(m, w_ref[...], preferred_element_type=_F32).astype(o_ref.dtype)


def _memkv(mem, g, w_kv):
    batch, mlen, _ = mem.shape
    return pl.pallas_call(
        _memkv_kernel,
        grid=(batch,),
        in_specs=[pl.BlockSpec((None, mlen, D_MODEL), lambda b: (b, 0, 0)),
                  _const_spec((1, D_MODEL)), _const_spec((D_MODEL, 2 * XA_WIDTH))],
        out_specs=pl.BlockSpec((None, mlen, 2 * XA_WIDTH), lambda b: (b, 0, 0)),
        out_shape=jax.ShapeDtypeStruct((batch, mlen, 2 * XA_WIDTH), _BF16),
        compiler_params=_params(("parallel",), 32),
        name="mem_kv",
    )(mem, g, w_kv)


def _mixout_ffn_stages(rows, x_ref, ymla_ref, ylru_ref, wout_ref, gmix_ref, gxa_ref, wq_ref,
                       kv_ref, wo_ref, gxo_ref, fgpre_ref, fwgu_ref, fwd_ref, fgpost_ref, o_ref):
    y = jnp.dot(ymla_ref[rows, :], wout_ref[0:MLA_WIDTH, :], preferred_element_type=_F32)
    y = y + jnp.dot(ylru_ref[rows, :], wout_ref[MLA_WIDTH:D_MODEL, :],
                    preferred_element_type=_F32)
    yield
    x = x_ref[rows, :] + _rmsnorm(y, gmix_ref[...])
    h = _rmsnorm(x, gxa_ref[...]).astype(_BF16)
    q = jnp.dot(h, wq_ref[...], preferred_element_type=_F32)
    q = (q * ((1.0 / math.sqrt(XA_HEAD_DIM)) * math.log2(math.e))).astype(_BF16)
    heads = []
    for hd in range(XA_HEADS):
        yield
        lo = hd * XA_HEAD_DIM
        kh = kv_ref[:, lo:lo + XA_HEAD_DIM]
        vh = kv_ref[:, XA_WIDTH + lo:XA_WIDTH + lo + XA_HEAD_DIM]
        s = lax.dot_general(q[:, lo:lo + XA_HEAD_DIM], kh, (((1,), (1,)), ((), ())),
                            preferred_element_type=_F32)
        e = jnp.exp2(s - jnp.max(s, axis=-1, keepdims=True))
        inv_sum = 1.0 / jnp.sum(e, axis=-1, keepdims=True)
        oh = jnp.dot(e.astype(_BF16), vh, preferred_element_type=_F32) * inv_sum
        heads.append(oh.astype(_BF16))
    yield
    o = jnp.concatenate(heads, axis=-1)
    y2 = jnp.dot(o, wo_ref[...], preferred_element_type=_F32)
    yield
    x = x + _rmsnorm(y2, gxo_ref[...])
    o_ref[rows, :] = yield from _ffn_stages(x, fgpre_ref, fwgu_ref, fwd_ref, fgpost_ref)


def _mixout_ffn_kernel(*refs):
    (*head, fgpre_ref, wgu_hbm, wd_hbm, fgpost_ref, o_ref,
     wgu_sc, wd_sc, stage_gu, stage_d, sem_gu, sem_d) = refs

    @pl.when(pl.program_id(0) == 0)
    def _():
        _stage_weight_bf16(wgu_hbm, wgu_sc, stage_gu, sem_gu)
        _stage_weight_bf16(wd_hbm, wd_sc, stage_d, sem_d)

    _alternate(lambda n, rows: _mixout_ffn_stages(rows, *head, fgpre_ref, wgu_sc, wd_sc,
                                                  fgpost_ref, o_ref), BIG_ROW_TILE)


def _mixout_ffn(x, y_mla, y_lru, w_out, g_mix, g_xa, w_q, kv, w_o, g_xo, ffn_args, seq):
    n = x.shape[0]
    nsb = seq // BIG_ROW_TILE
    mlen = kv.shape[1]
    row = lambda w: pl.BlockSpec((BIG_ROW_TILE, w), lambda i: (i, 0))
    hbm = pl.BlockSpec(memory_space=pl.ANY)
    return pl.pallas_call(
        _mixout_ffn_kernel,
        grid=(n // BIG_ROW_TILE,),
        in_specs=[row(D_MODEL), row(MLA_WIDTH), row(LRU_WIDTH),
                  _const_spec((D_MODEL, D_MODEL)), _const_spec((1, D_MODEL)),
                  _const_spec((1, D_MODEL)), _const_spec((D_MODEL, XA_WIDTH)),
                  pl.BlockSpec((None, mlen, 2 * XA_WIDTH), lambda i: (i // nsb, 0, 0)),
                  _const_spec((XA_WIDTH, D_MODEL)), _const_spec((1, D_MODEL)),
                  _const_spec((1, D_MODEL)), hbm, hbm, _const_spec((1, D_MODEL))],
        out_specs=row(D_MODEL),
        out_shape=jax.ShapeDtypeStruct((n, D_MODEL), _F32),
        scratch_shapes=_ffn_weight_scratch(),
        compiler_params=_params(("arbitrary",), 58),
        name="mixer_out_xattn_ffn",
    )(x, y_mla, y_lru, w_out, g_mix, g_xa, w_q, kv, w_o, g_xo, *ffn_args)


def _swap_halves(w):
    half = w.shape[-1] // 2
    return jnp.concatenate([w[..., half:], w[..., :half]], axis=-1)


def _prep_w_in(w_in):
    o1 = Q_LORA_RANK
    o2 = o1 + KV_LORA_RANK
    o3 = o2 + QK_ROPE_DIM
    o4 = o3 + LRU_WIDTH
    k_pe = w_in[:, o2:o3]
    return jnp.concatenate([w_in[:, :o1], w_in[:, o1:o2], w_in[:, o3:o4], w_in[:, o4:],
                            k_pe, _swap_halves(k_pe)], axis=-1).astype(_BF16)


def _prep_w_uq(w_uq):
    w = w_uq.reshape(Q_LORA_RANK, MLA_HEADS, QK_DIM)
    pe = w[..., QK_NOPE_DIM:]
    w = jnp.concatenate([w[..., :QK_NOPE_DIM], pe, _swap_halves(pe)], axis=-1)
    return w.reshape(Q_LORA_RANK, MLA_HEADS * QK_PAD).astype(_BF16)


def _prep_w_ukv(w_ukv):
    w = w_ukv.reshape(KV_LORA_RANK, MLA_HEADS, QK_NOPE_DIM + V_HEAD_DIM)
    w_k = w[..., :QK_NOPE_DIM].reshape(KV_LORA_RANK, MLA_HEADS * QK_NOPE_DIM)
    w_v = w[..., QK_NOPE_DIM:].reshape(KV_LORA_RANK, MLA_HEADS * V_HEAD_DIM)
    return w_k.astype(_BF16), w_v.T.astype(_BF16)


def _block_diag(w):
    nb, d, e = w.shape
    eye = jnp.eye(nb, dtype=w.dtype)
    return (eye[:, None, :, None] * w[:, :, None, :]).reshape(nb * d, nb * e)


def kernel(x, mem, positions, ffn1_pre_g, ffn1_w_gu, ffn1_w_down, ffn1_post_g, mix_pre_g, w_in, q_a_norm_g, w_uq, kv_a_norm_g, w_ukv, conv_w, conv_b, rg_w_a, rg_b_a, rg_w_x, rg_b_x, rg_lambda, w_out, mix_post_g, xa_pre_g, mem_norm_g, xa_w_q, xa_w_kv, xa_w_o, xa_post_g, ffn2_pre_g, ffn2_w_gu, ffn2_w_down, ffn2_post_g):
    batch, seq, _ = x.shape
    n = batch * seq
    depth = ffn1_pre_g.shape[0]
    bf = lambda w: w.astype(_BF16)
    vec = lambda g: g.reshape(1, -1)

    inv = ROPE_THETA ** (-jnp.arange(0, QK_ROPE_DIM, 2, dtype=_F32) / QK_ROPE_DIM)
    inv = jnp.tile(inv, 2 * LANES // QK_ROPE_DIM).reshape(1, LANES)
    pos = positions.reshape(n, 1)

    xf = x.reshape(n, D_MODEL)
    for l in range(depth):
        ffn2 = (vec(ffn2_pre_g[l]), ffn2_w_gu[l], ffn2_w_down[l], vec(ffn2_post_g[l]))
        w_k, w_vt = _prep_w_ukv(w_ukv[l])
        w_ax = bf(jnp.concatenate([_block_diag(rg_w_a[l]), _block_diag(rg_w_x[l])], axis=-1))
        b_ax = jnp.concatenate([rg_b_a[l].reshape(1, -1), rg_b_x[l].reshape(1, -1)], axis=-1)
        xf = _ffn(xf, vec(ffn1_pre_g[l]), ffn1_w_gu[l], ffn1_w_down[l], vec(ffn1_post_g[l]))
        q, k, vt, y_lru, kmax = _mixin_lru(
            xf, pos, inv, vec(mix_pre_g[l]), _prep_w_in(w_in[l]), vec(q_a_norm_g[l]),
            _prep_w_uq(w_uq[l]), vec(kv_a_norm_g[l]), w_k, w_vt, conv_w[l], vec(conv_b[l]),
            w_ax, b_ax, vec(rg_lambda[l]), batch, seq)
        y_mla = _attention(q, k, vt, kmax, batch, seq).reshape(n, MLA_WIDTH)

        kv = _memkv(mem, vec(mem_norm_g[l]), bf(xa_w_kv[l]))
        xf = _mixout_ffn(xf, y_mla, y_lru, bf(w_out[l]), vec(mix_post_g[l]), vec(xa_pre_g[l]),
                         bf(xa_w_q[l]), kv, bf(xa_w_o[l]), vec(xa_post_g[l]), ffn2, seq)
    return xf.reshape(batch, seq, D_MODEL)
```

```python
import functools
import math

import jax
import jax.numpy as jnp
from jax import lax
from jax.experimental import pallas as pl
from jax.experimental.pallas import tpu as pltpu

D_MODEL = 1024
MLA_HEADS = 4
QK_NOPE_DIM = 128
QK_ROPE_DIM = 64
QK_DIM = QK_NOPE_DIM + QK_ROPE_DIM
V_HEAD_DIM = 128
Q_LORA_RANK = 384
KV_LORA_RANK = 256
MLA_WIDTH = MLA_HEADS * V_HEAD_DIM
ROPE_THETA = 10000.0
LRU_WIDTH = D_MODEL - MLA_WIDTH
LRU_BLOCKS = 8
CONV_WIDTH = 4
LRU_C = 8.0
XA_HEADS = 4
XA_HEAD_DIM = 128
XA_WIDTH = XA_HEADS * XA_HEAD_DIM
D_FF = 2816
EPS = 1e-6
NEG_INF = -1e30

LANES = 128
SUBLANES = 8
MXU_DIM = 256
VMEM_BYTES_V7X = 64 * 1024 * 1024

QK_PAD = 2 * LANES

ROW_TILE = 512
BIG_ROW_TILE = 1024
ATTN_TILE = 512
FFN_CHUNKS = (768, 768, 768, 512)
assert sum(FFN_CHUNKS) == D_FF and all(c % MXU_DIM == 0 for c in FFN_CHUNKS)

_BF16 = jnp.bfloat16
_F32 = jnp.float32


def _rmsnorm(x, g):
    return x * lax.rsqrt(jnp.mean(x * x, axis=-1, keepdims=True) + EPS) * g


def _const_spec(shape):
    nd = len(shape)
    return pl.BlockSpec(shape, lambda *_: (0,) * nd, pipeline_mode=pl.Buffered(1))


def _params(semantics, vmem_mb):
    return pltpu.CompilerParams(dimension_semantics=semantics,
                                vmem_limit_bytes=vmem_mb * 1024 * 1024)


def _alternate(make_stages, tile_rows):
    subtiles = [make_stages(n, pl.ds(n * ROW_TILE, ROW_TILE)) for n in range(tile_rows // ROW_TILE)]
    while subtiles:
        for gen in list(subtiles):
            if next(gen, StopIteration) is StopIteration:
                subtiles.remove(gen)


def _ffn_stages(x, gpre_ref, wgu_ref, wd_ref, gpost_ref):
    h = _rmsnorm(x, gpre_ref[...]).astype(_BF16)
    acc = None
    off = 0
    for c in FFN_CHUNKS:
        g = jnp.dot(h, wgu_ref[:, off:off + c], preferred_element_type=_F32)
        u = jnp.dot(h, wgu_ref[:, D_FF + off:D_FF + off + c], preferred_element_type=_F32)
        a = (g * _sigmoid(g) * u).astype(_BF16)
        part = jnp.dot(a, wd_ref[off:off + c, :], preferred_element_type=_F32)
        acc = part if acc is None else acc + part
        off += c
        yield
    return x + 0.5 * _rmsnorm(acc, gpost_ref[...])


WEIGHT_STAGE_ROWS = {2 * D_FF: 64, D_MODEL: 176}
WEIGHT_STAGE_SLOTS = 4


def _ffn_weight_scratch():
    return [pltpu.VMEM((D_MODEL, 2 * D_FF), _BF16), pltpu.VMEM((D_FF, D_MODEL), _BF16),
            pltpu.VMEM((WEIGHT_STAGE_SLOTS, WEIGHT_STAGE_ROWS[2 * D_FF], 2 * D_FF), _F32),
            pltpu.VMEM((WEIGHT_STAGE_SLOTS, WEIGHT_STAGE_ROWS[D_MODEL], D_MODEL), _F32),
            pltpu.SemaphoreType.DMA((WEIGHT_STAGE_SLOTS,)),
            pltpu.SemaphoreType.DMA((WEIGHT_STAGE_SLOTS,))]


def _stage_weight_bf16(src_hbm, dst_sc, stage_sc, sem):
    slots, rows = stage_sc.shape[0], stage_sc.shape[1]
    chunks = src_hbm.shape[0] // rows
    assert chunks * rows == src_hbm.shape[0] and rows % (2 * SUBLANES) == 0

    def copy(c):
        return pltpu.make_async_copy(src_hbm.at[pl.ds(c * rows, rows), :],
                                     stage_sc.at[c % slots], sem.at[c % slots])

    for c in range(min(slots - 1, chunks)):
        copy(c).start()
    for c in range(chunks):
        if c + slots - 1 < chunks:
            copy(c + slots - 1).start()
        copy(c).wait()
        dst_sc[c * rows:(c + 1) * rows, :] = stage_sc[c % slots].astype(_BF16)


def _ffn_kernel(x_ref, gpre_ref, wgu_hbm, wd_hbm, gpost_ref, o_ref, wgu_sc, wd_sc, stage_gu,
                stage_d, sem_gu, sem_d):
    @pl.when(pl.program_id(0) == 0)
    def _():
        _stage_weight_bf16(wgu_hbm, wgu_sc, stage_gu, sem_gu)
        _stage_weight_bf16(wd_hbm, wd_sc, stage_d, sem_d)

    def stages(n, rows):
        o_ref[rows, :] = yield from _ffn_stages(x_ref[rows, :], gpre_ref, wgu_sc, wd_sc,
                                                gpost_ref)
    _alternate(stages, BIG_ROW_TILE)


def _ffn(x, g_pre, w_gu, w_down, g_post):
    n = x.shape[0]
    row = pl.BlockSpec((BIG_ROW_TILE, D_MODEL), lambda i: (i, 0))
    hbm = pl.BlockSpec(memory_space=pl.ANY)
    return pl.pallas_call(
        _ffn_kernel,
        grid=(n // BIG_ROW_TILE,),
        in_specs=[row, _const_spec((1, D_MODEL)), hbm, hbm, _const_spec((1, D_MODEL))],
        out_specs=row,
        out_shape=jax.ShapeDtypeStruct((n, D_MODEL), _F32),
        scratch_shapes=_ffn_weight_scratch(),
        compiler_params=_params(("arbitrary",), 56),
        name="ffn",
    )(x, g_pre, w_gu, w_down, g_post)


def _sigmoid(x):
    return 0.5 * jnp.tanh(0.5 * x) + 0.5


def _linear_recurrence(a, b, h0):
    t, c = a.shape
    groups = t // SUBLANES
    a = a.reshape(groups, SUBLANES, c)
    b = b.reshape(groups, SUBLANES, c)
    sub = lax.broadcasted_iota(jnp.int32, a.shape, 1)
    d = 1
    while d < SUBLANES:
        keep = sub >= d
        b = a * jnp.where(keep, pltpu.roll(b, d, axis=1), 0.0) + b
        a = a * jnp.where(keep, pltpu.roll(a, d, axis=1), 1.0)
        d *= 2
    out = []
    for g in range(groups):
        hg = b[g] + a[g] * h0
        out.append(hg)
        h0 = hg[SUBLANES - 1:SUBLANES, :]
    return jnp.concatenate(out, axis=0)


def _rglru(u, gate, cw_ref, cb_ref, wax_ref, bax_ref, lam_ref, tail_sc, h_sc):
    tail = tail_sc[...]
    row8 = lax.broadcasted_iota(jnp.int32, tail.shape, 0)
    xc = cb_ref[...] + u * cw_ref[CONV_WIDTH - 1:CONV_WIDTH, :]
    for d in range(1, CONV_WIDTH):
        rolled = pltpu.roll(u, d, axis=0)
        head = jnp.where(row8 < d, pltpu.roll(tail, d, axis=0), rolled[:SUBLANES])
        shifted = jnp.concatenate([head, rolled[SUBLANES:]], axis=0)
        xc = xc + shifted * cw_ref[CONV_WIDTH - 1 - d:CONV_WIDTH - d, :]
    tail_sc[...] = u[u.shape[0] - SUBLANES:, :]

    pre = jnp.dot(xc.astype(_BF16), wax_ref[...], preferred_element_type=_F32) + bax_ref[...]
    r = _sigmoid(pre[:, :LRU_WIDTH])
    i = _sigmoid(pre[:, LRU_WIDTH:])
    neg_lam = -lam_ref[...]
    softplus = jnp.maximum(neg_lam, 0.0) + jnp.log1p(jnp.exp(-jnp.abs(neg_lam)))
    log_a = -LRU_C * r * softplus
    a = jnp.exp(log_a)
    b = jnp.sqrt(-jnp.tanh(log_a) * (a * a + 1.0)) * (i * xc)

    hseq = _linear_recurrence(a, b, h_sc[...])
    h_sc[...] = hseq[hseq.shape[0] - 1:, :]
    return hseq * jax.nn.gelu(gate)


_EXP2_SCALE = (1.0 / math.sqrt(QK_DIM)) * math.log2(math.e)

_C_Q = 0
_C_KV = _C_Q + Q_LORA_RANK
_C_U = _C_KV + KV_LORA_RANK
_C_GATE = _C_U + LRU_WIDTH
_C_KPE = _C_GATE + LRU_WIDTH
_IN_COLS_EXT = _C_KPE + 2 * QK_ROPE_DIM


def _rope_table(pos_ref, rows, inv):
    freqs = QK_ROPE_DIM // 2
    parts = LANES // freqs
    quarter = rows.size // parts
    lane = lax.broadcasted_iota(jnp.int32, (quarter, LANES), 1)
    pos = None
    for g in range(parts):
        pg = pos_ref[pl.ds(rows.start + g * quarter, quarter), :].astype(_F32)
        pos = pg if pos is None else jnp.where(lane < g * freqs, pos, pg)
    ang = pos * inv
    cos, sin = jnp.cos(ang), jnp.sin(ang)
    out = []
    for g in range(parts):
        def placed(x, t):
            shift = (freqs * (t - g)) % LANES
            return pltpu.roll(x, shift, axis=1) if shift else x
        out.append(jnp.where(lane < freqs, placed(cos, 0),
                             jnp.where(lane < 2 * freqs, placed(cos, 1),
                                       jnp.where(lane < 3 * freqs, -placed(sin, 2),
                                                 placed(sin, 3)))))
    return jnp.concatenate(out, axis=0)


def _mixin_lru_stages(n, rows, x_ref, pos_ref, inv_ref, gpre_ref, win_ref, gq_ref, wuq_ref,
                      gkv_ref, wk_ref, wvt_ref, cw_ref, cb_ref, wax_ref, bax_ref, lam_ref,
                      q_ref, k_ref, vt_ref, ylru_ref, kmax_ref, tail_sc, h_sc):
    h = _rmsnorm(x_ref[rows, :], gpre_ref[...]).astype(_BF16)
    z = jnp.dot(h, win_ref[...], preferred_element_type=_F32)
    yield

    table = _rope_table(pos_ref, rows, inv_ref[...])
    lane = lax.broadcasted_iota(jnp.int32, table.shape, 1)
    low = (lane < QK_ROPE_DIM).astype(_F32)

    def rope(pair):
        prod = pair * table
        return prod + pltpu.roll(prod, QK_ROPE_DIM, axis=1)

    k_rot = rope(z[:, _C_KPE:_IN_COLS_EXT])
    k_rot_sq = jnp.sum(k_rot * k_rot, axis=1, keepdims=True)
    k_rot = k_rot.astype(_BF16)

    c_q = _rmsnorm(z[:, _C_Q:_C_KV], gq_ref[...]).astype(_BF16)
    q_all = jnp.dot(c_q, wuq_ref[...], preferred_element_type=_F32) * _EXP2_SCALE
    c_kv = _rmsnorm(z[:, _C_KV:_C_U], gkv_ref[...]).astype(_BF16)
    k_nope = jnp.dot(c_kv, wk_ref[...], preferred_element_type=_F32)
    v_t = lax.dot_general(wvt_ref[...], c_kv, (((1,), (1,)), ((), ())),
                          preferred_element_type=_F32)
    yield
    for hd in range(MLA_HEADS):
        qh = q_all[:, hd * QK_PAD:(hd + 1) * QK_PAD]
        q_ref[hd, rows, 0:LANES] = qh[:, 0:LANES].astype(_BF16)
        q_ref[hd, rows, LANES:QK_PAD] = (rope(qh[:, LANES:QK_PAD]) * low).astype(_BF16)
        kh = k_nope[:, hd * LANES:(hd + 1) * LANES]
        k_ref[hd, rows, 0:LANES] = kh.astype(_BF16)
        k_ref[hd, rows, LANES:QK_PAD] = k_rot
        vt_ref[hd, n] = v_t[hd * V_HEAD_DIM:(hd + 1) * V_HEAD_DIM, :].astype(_BF16)
        k_sq = jnp.max(jnp.sum(kh * kh, axis=1, keepdims=True) + k_rot_sq, axis=0, keepdims=True)
        kmax_ref[hd:hd + 1, :] = jnp.maximum(kmax_ref[hd:hd + 1, :], k_sq)
    yield
    ylru_ref[rows, :] = _rglru(z[:, _C_U:_C_GATE], z[:, _C_GATE:_C_KPE], cw_ref, cb_ref, wax_ref,
                               bax_ref, lam_ref, tail_sc, h_sc).astype(ylru_ref.dtype)


def _mixin_lru_kernel(*refs, tiles_per_batch):
    kmax_ref, tail_sc, h_sc = refs[-3:]

    @pl.when(pl.program_id(0) % tiles_per_batch == 0)
    def _():
        kmax_ref[...] = jnp.zeros(kmax_ref.shape, _F32)
        tail_sc[...] = jnp.zeros(tail_sc.shape, _F32)
        h_sc[...] = jnp.zeros(h_sc.shape, _F32)

    _alternate(lambda n, rows: _mixin_lru_stages(n, rows, *refs), BIG_ROW_TILE)


def _mixin_lru(x, pos, inv, g_pre, w_in, g_q, w_uq, g_kv, w_k, w_vt, conv_w, conv_b, w_ax, b_ax,
               lam, batch, seq):
    n = x.shape[0]
    nsb = seq // BIG_ROW_TILE
    assert ROW_TILE == ATTN_TILE
    blocks = BIG_ROW_TILE // ATTN_TILE
    row = lambda w: pl.BlockSpec((BIG_ROW_TILE, w), lambda i: (i, 0))
    head = lambda w: pl.BlockSpec((None, MLA_HEADS, BIG_ROW_TILE, w),
                                  lambda i: (i // nsb, 0, i % nsb, 0))
    return pl.pallas_call(
        functools.partial(_mixin_lru_kernel, tiles_per_batch=nsb),
        grid=(n // BIG_ROW_TILE,),
        in_specs=[row(D_MODEL), row(1), _const_spec((1, LANES)), _const_spec((1, D_MODEL)),
                  _const_spec((D_MODEL, _IN_COLS_EXT)), _const_spec((1, Q_LORA_RANK)),
                  _const_spec((Q_LORA_RANK, MLA_HEADS * QK_PAD)),
                  _const_spec((1, KV_LORA_RANK)),
                  _const_spec((KV_LORA_RANK, MLA_HEADS * QK_NOPE_DIM)),
                  _const_spec((MLA_HEADS * V_HEAD_DIM, KV_LORA_RANK)),
                  _const_spec((CONV_WIDTH, LRU_WIDTH)), _const_spec((1, LRU_WIDTH)),
                  _const_spec((LRU_WIDTH, 2 * LRU_WIDTH)), _const_spec((1, 2 * LRU_WIDTH)),
                  _const_spec((1, LRU_WIDTH))],
        out_specs=[head(QK_PAD), head(QK_PAD),
                   pl.BlockSpec((None, MLA_HEADS, blocks, V_HEAD_DIM, ATTN_TILE),
                                lambda i: (i // nsb, 0, i % nsb, 0, 0)),
                   row(LRU_WIDTH),
                   pl.BlockSpec((None, SUBLANES, LANES), lambda i: (i // nsb, 0, 0))],
        out_shape=[jax.ShapeDtypeStruct((batch, MLA_HEADS, seq, QK_PAD), _BF16),
                   jax.ShapeDtypeStruct((batch, MLA_HEADS, seq, QK_PAD), _BF16),
                   jax.ShapeDtypeStruct((batch, MLA_HEADS, seq // ATTN_TILE, V_HEAD_DIM,
                                         ATTN_TILE), _BF16),
                   jax.ShapeDtypeStruct((n, LRU_WIDTH), _BF16),
                   jax.ShapeDtypeStruct((batch, SUBLANES, LANES), _F32)],
        scratch_shapes=[pltpu.VMEM((SUBLANES, LRU_WIDTH), _F32), pltpu.VMEM((1, LRU_WIDTH), _F32)],
        compiler_params=_params(("arbitrary",), 56),
        name="mixer_in_rglru",
    )(x, pos, inv, g_pre, w_in, g_q, w_uq, g_kv, w_k, w_vt, conv_w, conv_b, w_ax, b_ax, lam)


SHIFT_HEADROOM = 90.0
FIXED_SHIFT_WIDTH = 4
NORM_MARGIN = 1.02


def _attn_kernel(q_ref, kblk_ref, vtblk_ref, kmax_ref, o_ref, k_ref, vt_ref, m_sc, l_sc, acc_sc):
    qi = pl.program_id(1)
    k_ref[:, pl.ds(pl.multiple_of(qi * ATTN_TILE, ATTN_TILE), ATTN_TILE), :] = kblk_ref[...]
    vt_ref[:, qi] = vtblk_ref[:, 0]

    def scores(j, nb, hd):
        k = k_ref[hd, pl.ds(pl.multiple_of(j * ATTN_TILE, ATTN_TILE), nb * ATTN_TILE), :]
        return lax.dot_general(k, q_ref[hd], (((1,), (1,)), ((), ())),
                               preferred_element_type=_F32)

    def values_t(j, nb, hd):
        return jnp.concatenate([vt_ref[hd, j + r] for r in range(nb)], axis=1)

    def run_items(items, softmax_update):
        s_next = scores(*items[0])
        pending = None
        for n, item in enumerate(items):
            s = s_next
            if n + 1 < len(items):
                s_next = scores(*items[n + 1])
            if pending is not None:
                pending()
            pending = softmax_update(*item, s)
        pending()

    def all_heads(j, nb):
        return [(j, nb, hd) for hd in range(MLA_HEADS)]

    def diagonal(j, nb, hd, s):
        key = lax.broadcasted_iota(jnp.int32, s.shape, 0)
        qry = lax.broadcasted_iota(jnp.int32, s.shape, 1)
        s = jnp.where(key <= qry, s, NEG_INF)
        m = jnp.max(s, axis=0, keepdims=True)
        p = jnp.exp2(s - m)
        m_sc[hd] = m
        l_sc[hd] = jnp.sum(p, axis=0, keepdims=True)
        p = p.astype(_BF16)

        def values():
            acc_sc[hd] = jnp.dot(values_t(j, nb, hd), p, preferred_element_type=_F32)
        return values

    def fixed_shift(j, nb, hd, s):
        p = jnp.exp2(s - m_sc[hd])
        l_sc[hd] = l_sc[hd] + jnp.sum(p, axis=0, keepdims=True)
        p = p.astype(_BF16)

        def values():
            acc_sc[hd] = acc_sc[hd] + jnp.dot(values_t(j, nb, hd), p,
                                              preferred_element_type=_F32)
        return values

    def running_max(j, nb, hd, s):
        m_old = m_sc[hd]
        m_new = jnp.maximum(m_old, jnp.max(s, axis=0, keepdims=True))
        p = jnp.exp2(s - m_new)
        alpha = jnp.exp2(m_old - m_new)
        l_sc[hd] = alpha * l_sc[hd] + jnp.sum(p, axis=0, keepdims=True)
        m_sc[hd] = m_new
        p = p.astype(_BF16)

        def values():
            pv = jnp.dot(values_t(j, nb, hd), p, preferred_element_type=_F32)
            acc_sc[hd] = alpha * acc_sc[hd] + pv
        return values

    run_items(all_heads(qi, 1), diagonal)

    excess = jnp.zeros((1, ATTN_TILE), _F32)
    ones = jnp.ones((2 * SUBLANES, QK_PAD), _BF16)
    for hd in range(MLA_HEADS):
        q = q_ref[hd]
        q_sq = lax.dot_general(ones, q * q, (((1,), (1,)), ((), ())),
                               preferred_element_type=_F32)[0:1]
        k_sq = jnp.max(kmax_ref[hd:hd + 1, :], axis=1, keepdims=True)
        bound = jnp.sqrt(q_sq * k_sq) * NORM_MARGIN
        excess = jnp.maximum(excess, bound - m_sc[hd])
    fixed_ok = jnp.max(excess) <= SHIFT_HEADROOM

    def loop_with(update, width):
        def body(t, carry):
            run_items(all_heads(t * width, width), update)
            return carry
        lax.fori_loop(0, qi // width, body, 0)
        part = width // 2
        while part >= 1:
            @pl.when(qi % (2 * part) >= part)
            def _(part=part):
                run_items(all_heads(qi - qi % (2 * part), part), update)
            part //= 2

    @pl.when(fixed_ok)
    def _():
        loop_with(fixed_shift, FIXED_SHIFT_WIDTH)

    @pl.when(jnp.logical_not(fixed_ok))
    def _():
        loop_with(running_max, 1)

    for hd in range(MLA_HEADS):
        out = acc_sc[hd] / l_sc[hd]
        o_ref[:, hd * V_HEAD_DIM:(hd + 1) * V_HEAD_DIM] = out.T.astype(o_ref.dtype)


def _attention(q, k, vt, kmax, batch, seq):
    nq = seq // ATTN_TILE
    return pl.pallas_call(
        _attn_kernel,
        grid=(batch, nq),
        in_specs=[pl.BlockSpec((None, MLA_HEADS, ATTN_TILE, QK_PAD), lambda b, i: (b, 0, i, 0)),
                  pl.BlockSpec((None, MLA_HEADS, ATTN_TILE, QK_PAD), lambda b, i: (b, 0, i, 0)),
                  pl.BlockSpec((None, MLA_HEADS, 1, V_HEAD_DIM, ATTN_TILE),
                               lambda b, i: (b, 0, i, 0, 0)),
                  pl.BlockSpec((None, SUBLANES, LANES), lambda b, i: (b, 0, 0))],
        out_specs=pl.BlockSpec((None, ATTN_TILE, MLA_WIDTH), lambda b, i: (b, i, 0)),
        out_shape=jax.ShapeDtypeStruct((batch, seq, MLA_WIDTH), _BF16),
        scratch_shapes=[pltpu.VMEM((MLA_HEADS, seq, QK_PAD), _BF16),
                        pltpu.VMEM((MLA_HEADS, nq, V_HEAD_DIM, ATTN_TILE), _BF16),
                        pltpu.VMEM((MLA_HEADS, 1, ATTN_TILE), _F32),
                        pltpu.VMEM((MLA_HEADS, 1, ATTN_TILE), _F32),
                        pltpu.VMEM((MLA_HEADS, V_HEAD_DIM, ATTN_TILE), _F32)],
        compiler_params=_params(("parallel", "arbitrary"), 48),
        name="mla_attention",
    )(q, k, vt, kmax)


def _memkv_kernel(mem_ref, g_ref, w_ref, o_ref):
    m = _rmsnorm(mem_ref[...], g_ref[...]).astype(_BF16)
    o_ref[...] = jnp.dot(m, w_ref[...], preferred_element_type=_F32).astype(o_ref.dtype)


def _memkv(mem, g, w_kv):
    batch, mlen, _ = mem.shape
    return pl.pallas_call(
        _memkv_kernel,
        grid=(batch,),
        in_specs=[pl.BlockSpec((None, mlen, D_MODEL), lambda b: (b, 0, 0)),
                  _const_spec((1, D_MODEL)), _const_spec((D_MODEL, 2 * XA_WIDTH))],
        out_specs=pl.BlockSpec((None, mlen, 2 * XA_WIDTH), lambda b: (b, 0, 0)),
        out_shape=jax.ShapeDtypeStruct((batch, mlen, 2 * XA_WIDTH), _BF16),
        compiler_params=_params(("parallel",), 32),
        name="mem_kv",
    )(mem, g, w_kv)


def _mixout_ffn_stages(rows, x_ref, ymla_ref, ylru_ref, wout_ref, gmix_ref, gxa_ref, wq_ref,
                       kv_ref, wo_ref, gxo_ref, fgpre_ref, fwgu_ref, fwd_ref, fgpost_ref, o_ref):
    y = jnp.dot(ymla_ref[rows, :], wout_ref[0:MLA_WIDTH, :], preferred_element_type=_F32)
    y = y + jnp.dot(ylru_ref[rows, :], wout_ref[MLA_WIDTH:D_MODEL, :],
                    preferred_element_type=_F32)
    yield
    x = x_ref[rows, :] + _rmsnorm(y, gmix_ref[...])
    h = _rmsnorm(x, gxa_ref[...]).astype(_BF16)
    q = jnp.dot(h, wq_ref[...], preferred_element_type=_F32)
    q = (q * ((1.0 / math.sqrt(XA_HEAD_DIM)) * math.log2(math.e))).astype(_BF16)
    heads = []
    for hd in range(XA_HEADS):
        yield
        lo = hd * XA_HEAD_DIM
        kh = kv_ref[:, lo:lo + XA_HEAD_DIM]
        vh = kv_ref[:, XA_WIDTH + lo:XA_WIDTH + lo + XA_HEAD_DIM]
        s = lax.dot_general(q[:, lo:lo + XA_HEAD_DIM], kh, (((1,), (1,)), ((), ())),
                            preferred_element_type=_F32)
        e = jnp.exp2(s - jnp.max(s, axis=-1, keepdims=True))
        inv_sum = 1.0 / jnp.sum(e, axis=-1, keepdims=True)
        oh = jnp.dot(e.astype(_BF16), vh, preferred_element_type=_F32) * inv_sum
        heads.append(oh.astype(_BF16))
    yield
    o = jnp.concatenate(heads, axis=-1)
    y2 = jnp.dot(o, wo_ref[...], preferred_element_type=_F32)
    yield
    x = x + _rmsnorm(y2, gxo_ref[...])
    o_ref[rows, :] = yield from _ffn_stages(x, fgpre_ref, fwgu_ref, fwd_ref, fgpost_ref)


def _mixout_ffn_kernel(*refs):
    _alternate(lambda n, rows: _mixout_ffn_stages(rows, *refs), BIG_ROW_TILE)


def _mixout_ffn(x, y_mla, y_lru, w_out, g_mix, g_xa, w_q, kv, w_o, g_xo, ffn_args, seq):
    n = x.shape[0]
    nsb = seq // BIG_ROW_TILE
    mlen = kv.shape[1]
    row = lambda w: pl.BlockSpec((BIG_ROW_TILE, w), lambda i: (i, 0))
    return pl.pallas_call(
        _mixout_ffn_kernel,
        grid=(n // BIG_ROW_TILE,),
        in_specs=[row(D_MODEL), row(MLA_WIDTH), row(LRU_WIDTH),
                  _const_spec((D_MODEL, D_MODEL)), _const_spec((1, D_MODEL)),
                  _const_spec((1, D_MODEL)), _const_spec((D_MODEL, XA_WIDTH)),
                  pl.BlockSpec((None, mlen, 2 * XA_WIDTH), lambda i: (i // nsb, 0, 0)),
                  _const_spec((XA_WIDTH, D_MODEL)), _const_spec((1, D_MODEL)),
                  _const_spec((1, D_MODEL)), _const_spec((D_MODEL, 2 * D_FF)),
                  _const_spec((D_FF, D_MODEL)), _const_spec((1, D_MODEL))],
        out_specs=row(D_MODEL),
        out_shape=jax.ShapeDtypeStruct((n, D_MODEL), _F32),
        compiler_params=_params(("parallel",), 56),
        name="mixer_out_xattn_ffn",
    )(x, y_mla, y_lru, w_out, g_mix, g_xa, w_q, kv, w_o, g_xo, *ffn_args)


def _swap_halves(w):
    half = w.shape[-1] // 2
    return jnp.concatenate([w[..., half:], w[..., :half]], axis=-1)


def _prep_w_in(w_in):
    o1 = Q_LORA_RANK
    o2 = o1 + KV_LORA_RANK
    o3 = o2 + QK_ROPE_DIM
    o4 = o3 + LRU_WIDTH
    k_pe = w_in[:, o2:o3]
    return jnp.concatenate([w_in[:, :o1], w_in[:, o1:o2], w_in[:, o3:o4], w_in[:, o4:],
                            k_pe, _swap_halves(k_pe)], axis=-1).astype(_BF16)


def _prep_w_uq(w_uq):
    w = w_uq.reshape(Q_LORA_RANK, MLA_HEADS, QK_DIM)
    pe = w[..., QK_NOPE_DIM:]
    w = jnp.concatenate([w[..., :QK_NOPE_DIM], pe, _swap_halves(pe)], axis=-1)
    return w.reshape(Q_LORA_RANK, MLA_HEADS * QK_PAD).astype(_BF16)


def _prep_w_ukv(w_ukv):
    w = w_ukv.reshape(KV_LORA_RANK, MLA_HEADS, QK_NOPE_DIM + V_HEAD_DIM)
    w_k = w[..., :QK_NOPE_DIM].reshape(KV_LORA_RANK, MLA_HEADS * QK_NOPE_DIM)
    w_v = w[..., QK_NOPE_DIM:].reshape(KV_LORA_RANK, MLA_HEADS * V_HEAD_DIM)
    return w_k.astype(_BF16), w_v.T.astype(_BF16)


def _block_diag(w):
    nb, d, e = w.shape
    eye = jnp.eye(nb, dtype=w.dtype)
    return (eye[:, None, :, None] * w[:, :, None, :]).reshape(nb * d, nb * e)


def kernel(x, mem, positions, ffn1_pre_g, ffn1_w_gu, ffn1_w_down, ffn1_post_g, mix_pre_g, w_in, q_a_norm_g, w_uq, kv_a_norm_g, w_ukv, conv_w, conv_b, rg_w_a, rg_b_a, rg_w_x, rg_b_x, rg_lambda, w_out, mix_post_g, xa_pre_g, mem_norm_g, xa_w_q, xa_w_kv, xa_w_o, xa_post_g, ffn2_pre_g, ffn2_w_gu, ffn2_w_down, ffn2_post_g):
    batch, seq, _ = x.shape
    n = batch * seq
    depth = ffn1_pre_g.shape[0]
    bf = lambda w: w.astype(_BF16)
    vec = lambda g: g.reshape(1, -1)

    inv = ROPE_THETA ** (-jnp.arange(0, QK_ROPE_DIM, 2, dtype=_F32) / QK_ROPE_DIM)
    inv = jnp.tile(inv, 2 * LANES // QK_ROPE_DIM).reshape(1, LANES)
    pos = positions.reshape(n, 1)

    xf = x.reshape(n, D_MODEL)
    for l in range(depth):
        ffn2 = (vec(ffn2_pre_g[l]), bf(ffn2_w_gu[l]), bf(ffn2_w_down[l]), vec(ffn2_post_g[l]))
        w_k, w_vt = _prep_w_ukv(w_ukv[l])
        w_ax = bf(jnp.concatenate([_block_diag(rg_w_a[l]), _block_diag(rg_w_x[l])], axis=-1))
        b_ax = jnp.concatenate([rg_b_a[l].reshape(1, -1), rg_b_x[l].reshape(1, -1)], axis=-1)
        xf = _ffn(xf, vec(ffn1_pre_g[l]), ffn1_w_gu[l], ffn1_w_down[l], vec(ffn1_post_g[l]))
        q, k, vt, y_lru, kmax = _mixin_lru(
            xf, pos, inv, vec(mix_pre_g[l]), _prep_w_in(w_in[l]), vec(q_a_norm_g[l]),
            _prep_w_uq(w_uq[l]), vec(kv_a_norm_g[l]), w_k, w_vt, conv_w[l], vec(conv_b[l]),
            w_ax, b_ax, vec(rg_lambda[l]), batch, seq)
        y_mla = _attention(q, k, vt, kmax, batch, seq).reshape(n, MLA_WIDTH)

        kv = _memkv(mem, vec(mem_norm_g[l]), bf(xa_w_kv[l]))
        xf = _mixout_ffn(xf, y_mla, y_lru, bf(w_out[l]), vec(mix_post_g[l]), vec(xa_pre_g[l]),
                         bf(xa_w_q[l]), kv, bf(xa_w_o[l]), vec(xa_post_g[l]), ffn2, seq)
    return xf.reshape(batch, seq, D_MODEL)
```

```python
import functools
import math

import jax
import jax.numpy as jnp
from jax import lax
from jax.experimental import pallas as pl
from jax.experimental.pallas import tpu as pltpu

D_MODEL = 1024
MLA_HEADS = 4
QK_NOPE_DIM = 128
QK_ROPE_DIM = 64
QK_DIM = QK_NOPE_DIM + QK_ROPE_DIM
V_HEAD_DIM = 128
Q_LORA_RANK = 384
KV_LORA_RANK = 256
MLA_WIDTH = MLA_HEADS * V_HEAD_DIM
ROPE_THETA = 10000.0
LRU_WIDTH = D_MODEL - MLA_WIDTH
LRU_BLOCKS = 8
CONV_WIDTH = 4
LRU_C = 8.0
XA_HEADS = 4
XA_HEAD_DIM = 128
XA_WIDTH = XA_HEADS * XA_HEAD_DIM
D_FF = 2816
EPS = 1e-6
NEG_INF = -1e30

LANES = 128
SUBLANES = 8
MXU_DIM = 256
VMEM_BYTES_V7X = 64 * 1024 * 1024

QK_PAD = 2 * LANES

ROW_TILE = 512
BIG_ROW_TILE = 1024
ATTN_TILE = 512
FFN_CHUNKS = (768, 768, 768, 512)
assert sum(FFN_CHUNKS) == D_FF and all(c % MXU_DIM == 0 for c in FFN_CHUNKS)

_BF16 = jnp.bfloat16
_F32 = jnp.float32


def _rmsnorm(x, g):
    return x * lax.rsqrt(jnp.mean(x * x, axis=-1, keepdims=True) + EPS) * g


def _const_spec(shape):
    nd = len(shape)
    return pl.BlockSpec(shape, lambda *_: (0,) * nd, pipeline_mode=pl.Buffered(1))


def _params(semantics, vmem_mb):
    return pltpu.CompilerParams(dimension_semantics=semantics,
                                vmem_limit_bytes=vmem_mb * 1024 * 1024)


def _alternate(make_stages, tile_rows):
    subtiles = [make_stages(n, pl.ds(n * ROW_TILE, ROW_TILE)) for n in range(tile_rows // ROW_TILE)]
    while subtiles:
        for gen in list(subtiles):
            if next(gen, StopIteration) is StopIteration:
                subtiles.remove(gen)


def _ffn_stages(x, gpre_ref, wgu_ref, wd_ref, gpost_ref):
    h = _rmsnorm(x, gpre_ref[...]).astype(_BF16)
    acc = None
    off = 0
    for c in FFN_CHUNKS:
        g = jnp.dot(h, wgu_ref[:, off:off + c], preferred_element_type=_F32)
        u = jnp.dot(h, wgu_ref[:, D_FF + off:D_FF + off + c], preferred_element_type=_F32)
        a = (g * _sigmoid(g) * u).astype(_BF16)
        part = jnp.dot(a, wd_ref[off:off + c, :], preferred_element_type=_F32)
        acc = part if acc is None else acc + part
        off += c
        yield
    return x + 0.5 * _rmsnorm(acc, gpost_ref[...])


WEIGHT_STAGE_SLOTS = 4
FFN_STAGE_ROWS = (64, 176)
MIXOUT_STAGE_ROWS = (16, 64)


def _ffn_weight_scratch(gu_rows, down_rows):
    return [pltpu.VMEM((D_MODEL, 2 * D_FF), _BF16), pltpu.VMEM((D_FF, D_MODEL), _BF16),
            pltpu.VMEM((WEIGHT_STAGE_SLOTS, gu_rows, 2 * D_FF), _F32),
            pltpu.VMEM((WEIGHT_STAGE_SLOTS, down_rows, D_MODEL), _F32),
            pltpu.SemaphoreType.DMA((WEIGHT_STAGE_SLOTS,)),
            pltpu.SemaphoreType.DMA((WEIGHT_STAGE_SLOTS,))]


def _stage_weight_bf16(src_hbm, dst_sc, stage_sc, sem):
    slots, rows = stage_sc.shape[0], stage_sc.shape[1]
    chunks = src_hbm.shape[0] // rows
    assert chunks * rows == src_hbm.shape[0] and rows % (2 * SUBLANES) == 0

    def copy(c):
        return pltpu.make_async_copy(src_hbm.at[pl.ds(c * rows, rows), :],
                                     stage_sc.at[c % slots], sem.at[c % slots])

    for c in range(min(slots - 1, chunks)):
        copy(c).start()
    for c in range(chunks):
        if c + slots - 1 < chunks:
            copy(c + slots - 1).start()
        copy(c).wait()
        dst_sc[c * rows:(c + 1) * rows, :] = stage_sc[c % slots].astype(_BF16)


def _ffn_kernel(x_ref, gpre_ref, wgu_hbm, wd_hbm, gpost_ref, o_ref, wgu_sc, wd_sc, stage_gu,
                stage_d, sem_gu, sem_d):
    @pl.when(pl.program_id(0) == 0)
    def _():
        _stage_weight_bf16(wgu_hbm, wgu_sc, stage_gu, sem_gu)
        _stage_weight_bf16(wd_hbm, wd_sc, stage_d, sem_d)

    def stages(n, rows):
        o_ref[rows, :] = yield from _ffn_stages(x_ref[rows, :], gpre_ref, wgu_sc, wd_sc,
                                                gpost_ref)
    _alternate(stages, BIG_ROW_TILE)


def _ffn(x, g_pre, w_gu, w_down, g_post):
    n = x.shape[0]
    row = pl.BlockSpec((BIG_ROW_TILE, D_MODEL), lambda i: (i, 0))
    hbm = pl.BlockSpec(memory_space=pl.ANY)
    return pl.pallas_call(
        _ffn_kernel,
        grid=(n // BIG_ROW_TILE,),
        in_specs=[row, _const_spec((1, D_MODEL)), hbm, hbm, _const_spec((1, D_MODEL))],
        out_specs=row,
        out_shape=jax.ShapeDtypeStruct((n, D_MODEL), _F32),
        scratch_shapes=_ffn_weight_scratch(*FFN_STAGE_ROWS),
        compiler_params=_params(("arbitrary",), 56),
        name="ffn",
    )(x, g_pre, w_gu, w_down, g_post)


def _sigmoid(x):
    return 0.5 * jnp.tanh(0.5 * x) + 0.5


def _linear_recurrence(a, b, h0):
    t, c = a.shape
    groups = t // SUBLANES
    a = a.reshape(groups, SUBLANES, c)
    b = b.reshape(groups, SUBLANES, c)
    sub = lax.broadcasted_iota(jnp.int32, a.shape, 1)
    d = 1
    while d < SUBLANES:
        keep = sub >= d
        b = a * jnp.where(keep, pltpu.roll(b, d, axis=1), 0.0) + b
        a = a * jnp.where(keep, pltpu.roll(a, d, axis=1), 1.0)
        d *= 2
    out = []
    for g in range(groups):
        hg = b[g] + a[g] * h0
        out.append(hg)
        h0 = hg[SUBLANES - 1:SUBLANES, :]
    return jnp.concatenate(out, axis=0)


def _rglru(u, gate, cw_ref, cb_ref, wax_ref, bax_ref, lam_ref, tail_sc, h_sc):
    tail = tail_sc[...]
    row8 = lax.broadcasted_iota(jnp.int32, tail.shape, 0)
    xc = cb_ref[...] + u * cw_ref[CONV_WIDTH - 1:CONV_WIDTH, :]
    for d in range(1, CONV_WIDTH):
        rolled = pltpu.roll(u, d, axis=0)
        head = jnp.where(row8 < d, pltpu.roll(tail, d, axis=0), rolled[:SUBLANES])
        shifted = jnp.concatenate([head, rolled[SUBLANES:]], axis=0)
        xc = xc + shifted * cw_ref[CONV_WIDTH - 1 - d:CONV_WIDTH - d, :]
    tail_sc[...] = u[u.shape[0] - SUBLANES:, :]

    pre = jnp.dot(xc.astype(_BF16), wax_ref[...], preferred_element_type=_F32) + bax_ref[...]
    r = _sigmoid(pre[:, :LRU_WIDTH])
    i = _sigmoid(pre[:, LRU_WIDTH:])
    neg_lam = -lam_ref[...]
    softplus = jnp.maximum(neg_lam, 0.0) + jnp.log1p(jnp.exp(-jnp.abs(neg_lam)))
    log_a = -LRU_C * r * softplus
    a = jnp.exp(log_a)
    b = jnp.sqrt(-jnp.tanh(log_a) * (a * a + 1.0)) * (i * xc)

    hseq = _linear_recurrence(a, b, h_sc[...])
    h_sc[...] = hseq[hseq.shape[0] - 1:, :]
    return hseq * jax.nn.gelu(gate)


_EXP2_SCALE = (1.0 / math.sqrt(QK_DIM)) * math.log2(math.e)

_C_Q = 0
_C_KV = _C_Q + Q_LORA_RANK
_C_U = _C_KV + KV_LORA_RANK
_C_GATE = _C_U + LRU_WIDTH
_C_KPE = _C_GATE + LRU_WIDTH
_IN_COLS_EXT = _C_KPE + 2 * QK_ROPE_DIM


def _rope_table(pos_ref, rows, inv):
    freqs = QK_ROPE_DIM // 2
    parts = LANES // freqs
    quarter = rows.size // parts
    lane = lax.broadcasted_iota(jnp.int32, (quarter, LANES), 1)
    pos = None
    for g in range(parts):
        pg = pos_ref[pl.ds(rows.start + g * quarter, quarter), :].astype(_F32)
        pos = pg if pos is None else jnp.where(lane < g * freqs, pos, pg)
    ang = pos * inv
    cos, sin = jnp.cos(ang), jnp.sin(ang)
    out = []
    for g in range(parts):
        def placed(x, t):
            shift = (freqs * (t - g)) % LANES
            return pltpu.roll(x, shift, axis=1) if shift else x
        out.append(jnp.where(lane < freqs, placed(cos, 0),
                             jnp.where(lane < 2 * freqs, placed(cos, 1),
                                       jnp.where(lane < 3 * freqs, -placed(sin, 2),
                                                 placed(sin, 3)))))
    return jnp.concatenate(out, axis=0)


def _mixin_lru_stages(n, rows, x_ref, pos_ref, inv_ref, gpre_ref, win_ref, gq_ref, wuq_ref,
                      gkv_ref, wk_ref, wvt_ref, cw_ref, cb_ref, wax_ref, bax_ref, lam_ref,
                      q_ref, k_ref, vt_ref, ylru_ref, kmax_ref, tail_sc, h_sc):
    h = _rmsnorm(x_ref[rows, :], gpre_ref[...]).astype(_BF16)
    z = jnp.dot(h, win_ref[...], preferred_element_type=_F32)
    yield

    table = _rope_table(pos_ref, rows, inv_ref[...])
    lane = lax.broadcasted_iota(jnp.int32, table.shape, 1)
    low = (lane < QK_ROPE_DIM).astype(_F32)

    def rope(pair):
        prod = pair * table
        return prod + pltpu.roll(prod, QK_ROPE_DIM, axis=1)

    k_rot = rope(z[:, _C_KPE:_IN_COLS_EXT])
    k_rot_sq = jnp.sum(k_rot * k_rot, axis=1, keepdims=True)
    k_rot = k_rot.astype(_BF16)

    c_q = _rmsnorm(z[:, _C_Q:_C_KV], gq_ref[...]).astype(_BF16)
    q_all = jnp.dot(c_q, wuq_ref[...], preferred_element_type=_F32) * _EXP2_SCALE
    c_kv = _rmsnorm(z[:, _C_KV:_C_U], gkv_ref[...]).astype(_BF16)
    k_nope = jnp.dot(c_kv, wk_ref[...], preferred_element_type=_F32)
    v_t = lax.dot_general(wvt_ref[...], c_kv, (((1,), (1,)), ((), ())),
                          preferred_element_type=_F32)
    yield
    for hd in range(MLA_HEADS):
        qh = q_all[:, hd * QK_PAD:(hd + 1) * QK_PAD]
        q_ref[hd, rows, 0:LANES] = qh[:, 0:LANES].astype(_BF16)
        q_ref[hd, rows, LANES:QK_PAD] = (rope(qh[:, LANES:QK_PAD]) * low).astype(_BF16)
        kh = k_nope[:, hd * LANES:(hd + 1) * LANES]
        k_ref[hd, rows, 0:LANES] = kh.astype(_BF16)
        k_ref[hd, rows, LANES:QK_PAD] = k_rot
        vt_ref[hd, n] = v_t[hd * V_HEAD_DIM:(hd + 1) * V_HEAD_DIM, :].astype(_BF16)
        k_sq = jnp.max(jnp.sum(kh * kh, axis=1, keepdims=True) + k_rot_sq, axis=0, keepdims=True)
        kmax_ref[hd:hd + 1, :] = jnp.maximum(kmax_ref[hd:hd + 1, :], k_sq)
    yield
    ylru_ref[rows, :] = _rglru(z[:, _C_U:_C_GATE], z[:, _C_GATE:_C_KPE], cw_ref, cb_ref, wax_ref,
                               bax_ref, lam_ref, tail_sc, h_sc).astype(ylru_ref.dtype)


def _mixin_lru_kernel(*refs, tiles_per_batch):
    kmax_ref, tail_sc, h_sc = refs[-3:]

    @pl.when(pl.program_id(0) % tiles_per_batch == 0)
    def _():
        kmax_ref[...] = jnp.zeros(kmax_ref.shape, _F32)
        tail_sc[...] = jnp.zeros(tail_sc.shape, _F32)
        h_sc[...] = jnp.zeros(h_sc.shape, _F32)

    _alternate(lambda n, rows: _mixin_lru_stages(n, rows, *refs), BIG_ROW_TILE)


def _mixin_lru(x, pos, inv, g_pre, w_in, g_q, w_uq, g_kv, w_k, w_vt, conv_w, conv_b, w_ax, b_ax,
               lam, batch, seq):
    n = x.shape[0]
    nsb = seq // BIG_ROW_TILE
    assert ROW_TILE == ATTN_TILE
    blocks = BIG_ROW_TILE // ATTN_TILE
    row = lambda w: pl.BlockSpec((BIG_ROW_TILE, w), lambda i: (i, 0))
    head = lambda w: pl.BlockSpec((None, MLA_HEADS, BIG_ROW_TILE, w),
                                  lambda i: (i // nsb, 0, i % nsb, 0))
    return pl.pallas_call(
        functools.partial(_mixin_lru_kernel, tiles_per_batch=nsb),
        grid=(n // BIG_ROW_TILE,),
        in_specs=[row(D_MODEL), row(1), _const_spec((1, LANES)), _const_spec((1, D_MODEL)),
                  _const_spec((D_MODEL, _IN_COLS_EXT)), _const_spec((1, Q_LORA_RANK)),
                  _const_spec((Q_LORA_RANK, MLA_HEADS * QK_PAD)),
                  _const_spec((1, KV_LORA_RANK)),
                  _const_spec((KV_LORA_RANK, MLA_HEADS * QK_NOPE_DIM)),
                  _const_spec((MLA_HEADS * V_HEAD_DIM, KV_LORA_RANK)),
                  _const_spec((CONV_WIDTH, LRU_WIDTH)), _const_spec((1, LRU_WIDTH)),
                  _const_spec((LRU_WIDTH, 2 * LRU_WIDTH)), _const_spec((1, 2 * LRU_WIDTH)),
                  _const_spec((1, LRU_WIDTH))],
        out_specs=[head(QK_PAD), head(QK_PAD),
                   pl.BlockSpec((None, MLA_HEADS, blocks, V_HEAD_DIM, ATTN_TILE),
                                lambda i: (i // nsb, 0, i % nsb, 0, 0)),
                   row(LRU_WIDTH),
                   pl.BlockSpec((None, SUBLANES, LANES), lambda i: (i // nsb, 0, 0))],
        out_shape=[jax.ShapeDtypeStruct((batch, MLA_HEADS, seq, QK_PAD), _BF16),
                   jax.ShapeDtypeStruct((batch, MLA_HEADS, seq, QK_PAD), _BF16),
                   jax.ShapeDtypeStruct((batch, MLA_HEADS, seq // ATTN_TILE, V_HEAD_DIM,
                                         ATTN_TILE), _BF16),
                   jax.ShapeDtypeStruct((n, LRU_WIDTH), _BF16),
                   jax.ShapeDtypeStruct((batch, SUBLANES, LANES), _F32)],
        scratch_shapes=[pltpu.VMEM((SUBLANES, LRU_WIDTH), _F32), pltpu.VMEM((1, LRU_WIDTH), _F32)],
        compiler_params=_params(("arbitrary",), 56),
        name="mixer_in_rglru",
    )(x, pos, inv, g_pre, w_in, g_q, w_uq, g_kv, w_k, w_vt, conv_w, conv_b, w_ax, b_ax, lam)


SHIFT_HEADROOM = 90.0
FIXED_SHIFT_WIDTH = 4
NORM_MARGIN = 1.02


def _attn_kernel(q_ref, kblk_ref, vtblk_ref, kmax_ref, o_ref, k_ref, vt_ref, m_sc, l_sc, acc_sc):
    qi = pl.program_id(1)
    k_ref[:, pl.ds(pl.multiple_of(qi * ATTN_TILE, ATTN_TILE), ATTN_TILE), :] = kblk_ref[...]
    vt_ref[:, qi] = vtblk_ref[:, 0]

    def scores(j, nb, hd):
        k = k_ref[hd, pl.ds(pl.multiple_of(j * ATTN_TILE, ATTN_TILE), nb * ATTN_TILE), :]
        return lax.dot_general(k, q_ref[hd], (((1,), (1,)), ((), ())),
                               preferred_element_type=_F32)

    def values_t(j, nb, hd):
        return jnp.concatenate([vt_ref[hd, j + r] for r in range(nb)], axis=1)

    def run_items(items, softmax_update):
        s_next = scores(*items[0])
        pending = None
        for n, item in enumerate(items):
            s = s_next
            if n + 1 < len(items):
                s_next = scores(*items[n + 1])
            if pending is not None:
                pending()
            pending = softmax_update(*item, s)
        pending()

    def all_heads(j, nb):
        return [(j, nb, hd) for hd in range(MLA_HEADS)]

    def diagonal(j, nb, hd, s):
        key = lax.broadcasted_iota(jnp.int32, s.shape, 0)
        qry = lax.broadcasted_iota(jnp.int32, s.shape, 1)
        s = jnp.where(key <= qry, s, NEG_INF)
        m = jnp.max(s, axis=0, keepdims=True)
        p = jnp.exp2(s - m)
        m_sc[hd] = m
        l_sc[hd] = jnp.sum(p, axis=0, keepdims=True)
        p = p.astype(_BF16)

        def values():
            acc_sc[hd] = jnp.dot(values_t(j, nb, hd), p, preferred_element_type=_F32)
        return values

    def fixed_shift(j, nb, hd, s):
        p = jnp.exp2(s - m_sc[hd])
        l_sc[hd] = l_sc[hd] + jnp.sum(p, axis=0, keepdims=True)
        p = p.astype(_BF16)

        def values():
            acc_sc[hd] = acc_sc[hd] + jnp.dot(values_t(j, nb, hd), p,
                                              preferred_element_type=_F32)
        return values

    def running_max(j, nb, hd, s):
        m_old = m_sc[hd]
        m_new = jnp.maximum(m_old, jnp.max(s, axis=0, keepdims=True))
        p = jnp.exp2(s - m_new)
        alpha = jnp.exp2(m_old - m_new)
        l_sc[hd] = alpha * l_sc[hd] + jnp.sum(p, axis=0, keepdims=True)
        m_sc[hd] = m_new
        p = p.astype(_BF16)

        def values():
            pv = jnp.dot(values_t(j, nb, hd), p, preferred_element_type=_F32)
            acc_sc[hd] = alpha * acc_sc[hd] + pv
        return values

    run_items(all_heads(qi, 1), diagonal)

    excess = jnp.zeros((1, ATTN_TILE), _F32)
    ones = jnp.ones((2 * SUBLANES, QK_PAD), _BF16)
    for hd in range(MLA_HEADS):
        q = q_ref[hd]
        q_sq = lax.dot_general(ones, q * q, (((1,), (1,)), ((), ())),
                               preferred_element_type=_F32)[0:1]
        k_sq = jnp.max(kmax_ref[hd:hd + 1, :], axis=1, keepdims=True)
        bound = jnp.sqrt(q_sq * k_sq) * NORM_MARGIN
        excess = jnp.maximum(excess, bound - m_sc[hd])
    fixed_ok = jnp.max(excess) <= SHIFT_HEADROOM

    def loop_with(update, width):
        def body(t, carry):
            run_items(all_heads(t * width, width), update)
            return carry
        lax.fori_loop(0, qi // width, body, 0)
        part = width // 2
        while part >= 1:
            @pl.when(qi % (2 * part) >= part)
            def _(part=part):
                run_items(all_heads(qi - qi % (2 * part), part), update)
            part //= 2

    @pl.when(fixed_ok)
    def _():
        loop_with(fixed_shift, FIXED_SHIFT_WIDTH)

    @pl.when(jnp.logical_not(fixed_ok))
    def _():
        loop_with(running_max, 1)

    for hd in range(MLA_HEADS):
        out = acc_sc[hd] / l_sc[hd]
        o_ref[:, hd * V_HEAD_DIM:(hd + 1) * V_HEAD_DIM] = out.T.astype(o_ref.dtype)


def _attention(q, k, vt, kmax, batch, seq):
    nq = seq // ATTN_TILE
    return pl.pallas_call(
        _attn_kernel,
        grid=(batch, nq),
        in_specs=[pl.BlockSpec((None, MLA_HEADS, ATTN_TILE, QK_PAD), lambda b, i: (b, 0, i, 0)),
                  pl.BlockSpec((None, MLA_HEADS, ATTN_TILE, QK_PAD), lambda b, i: (b, 0, i, 0)),
                  pl.BlockSpec((None, MLA_HEADS, 1, V_HEAD_DIM, ATTN_TILE),
                               lambda b, i: (b, 0, i, 0, 0)),
                  pl.BlockSpec((None, SUBLANES, LANES), lambda b, i: (b, 0, 0))],
        out_specs=pl.BlockSpec((None, ATTN_TILE, MLA_WIDTH), lambda b, i: (b, i, 0)),
        out_shape=jax.ShapeDtypeStruct((batch, seq, MLA_WIDTH), _BF16),
        scratch_shapes=[pltpu.VMEM((MLA_HEADS, seq, QK_PAD), _BF16),
                        pltpu.VMEM((MLA_HEADS, nq, V_HEAD_DIM, ATTN_TILE), _BF16),
                        pltpu.VMEM((MLA_HEADS, 1, ATTN_TILE), _F32),
                        pltpu.VMEM((MLA_HEADS, 1, ATTN_TILE), _F32),
                        pltpu.VMEM((MLA_HEADS, V_HEAD_DIM, ATTN_TILE), _F32)],
        compiler_params=_params(("parallel", "arbitrary"), 48),
        name="mla_attention",
    )(q, k, vt, kmax)


def _memkv_kernel(mem_ref, g_ref, w_ref, o_ref):
    m = _rmsnorm(mem_ref[...], g_ref[...]).astype(_BF16)
    o_ref[...] = jnp.dot(m, w_ref[...], preferred_element_type=_F32).astype(o_ref.dtype)


def _memkv(mem, g, w_kv):
    batch, mlen, _ = mem.shape
    return pl.pallas_call(
        _memkv_kernel,
        grid=(batch,),
        in_specs=[pl.BlockSpec((None, mlen, D_MODEL), lambda b: (b, 0, 0)),
                  _const_spec((1, D_MODEL)), _const_spec((D_MODEL, 2 * XA_WIDTH))],
        out_specs=pl.BlockSpec((None, mlen, 2 * XA_WIDTH), lambda b: (b, 0, 0)),
        out_shape=jax.ShapeDtypeStruct((batch, mlen, 2 * XA_WIDTH), _BF16),
        compiler_params=_params(("parallel",), 32),
        name="mem_kv",
    )(mem, g, w_kv)


def _mixout_ffn_stages(rows, x_ref, ymla_ref, ylru_ref, wout_ref, gmix_ref, gxa_ref, wq_ref,
                       kv_ref, wo_ref, gxo_ref, fgpre_ref, fwgu_ref, fwd_ref, fgpost_ref, o_ref):
    y = jnp.dot(ymla_ref[rows, :], wout_ref[0:MLA_WIDTH, :], preferred_element_type=_F32)
    y = y + jnp.dot(ylru_ref[rows, :], wout_ref[MLA_WIDTH:D_MODEL, :],
                    preferred_element_type=_F32)
    yield
    x = x_ref[rows, :] + _rmsnorm(y, gmix_ref[...])
    h = _rmsnorm(x, gxa_ref[...]).astype(_BF16)
    q = jnp.dot(h, wq_ref[...], preferred_element_type=_F32)
    q = (q * ((1.0 / math.sqrt(XA_HEAD_DIM)) * math.log2(math.e))).astype(_BF16)
    heads = []
    for hd in range(XA_HEADS):
        yield
        lo = hd * XA_HEAD_DIM
        kh = kv_ref[:, lo:lo + XA_HEAD_DIM]
        vh = kv_ref[:, XA_WIDTH + lo:XA_WIDTH + lo + XA_HEAD_DIM]
        s = lax.dot_general(q[:, lo:lo + XA_HEAD_DIM], kh, (((1,), (1,)), ((), ())),
                            preferred_element_type=_F32)
        e = jnp.exp2(s - jnp.max(s, axis=-1, keepdims=True))
        inv_sum = 1.0 / jnp.sum(e, axis=-1, keepdims=True)
        oh = jnp.dot(e.astype(_BF16), vh, preferred_element_type=_F32) * inv_sum
        heads.append(oh.astype(_BF16))
    yield
    o = jnp.concatenate(heads, axis=-1)
    y2 = jnp.dot(o, wo_ref[...], preferred_element_type=_F32)
    yield
    x = x + _rmsnorm(y2, gxo_ref[...])
    o_ref[rows, :] = yield from _ffn_stages(x, fgpre_ref, fwgu_ref, fwd_ref, fgpost_ref)


def _mixout_ffn_kernel(*refs):
    (*head, fgpre_ref, wgu_hbm, wd_hbm, fgpost_ref, o_ref,
     wgu_sc, wd_sc, stage_gu, stage_d, sem_gu, sem_d) = refs

    @pl.when(pl.program_id(0) == 0)
    def _():
        _stage_weight_bf16(wgu_hbm, wgu_sc, stage_gu, sem_gu)
        _stage_weight_bf16(wd_hbm, wd_sc, stage_d, sem_d)

    _alternate(lambda n, rows: _mixout_ffn_stages(rows, *head, fgpre_ref, wgu_sc, wd_sc,
                                                  fgpost_ref, o_ref), BIG_ROW_TILE)


def _mixout_ffn(x, y_mla, y_lru, w_out, g_mix, g_xa, w_q, kv, w_o, g_xo, ffn_args, seq):
    n = x.shape[0]
    nsb = seq // BIG_ROW_TILE
    mlen = kv.shape[1]
    row = lambda w: pl.BlockSpec((BIG_ROW_TILE, w), lambda i: (i, 0))
    hbm = pl.BlockSpec(memory_space=pl.ANY)
    return pl.pallas_call(
        _mixout_ffn_kernel,
        grid=(n // BIG_ROW_TILE,),
        in_specs=[row(D_MODEL), row(MLA_WIDTH), row(LRU_WIDTH),
                  _const_spec((D_MODEL, D_MODEL)), _const_spec((1, D_MODEL)),
                  _const_spec((1, D_MODEL)), _const_spec((D_MODEL, XA_WIDTH)),
                  pl.BlockSpec((None, mlen, 2 * XA_WIDTH), lambda i: (i // nsb, 0, 0)),
                  _const_spec((XA_WIDTH, D_MODEL)), _const_spec((1, D_MODEL)),
                  _const_spec((1, D_MODEL)), hbm, hbm, _const_spec((1, D_MODEL))],
        out_specs=row(D_MODEL),
        out_shape=jax.ShapeDtypeStruct((n, D_MODEL), _F32),
        scratch_shapes=_ffn_weight_scratch(*MIXOUT_STAGE_ROWS),
        compiler_params=_params(("arbitrary",), 58),
        name="mixer_out_xattn_ffn",
    )(x, y_mla, y_lru, w_out, g_mix, g_xa, w_q, kv, w_o, g_xo, *ffn_args)


def _swap_halves(w):
    half = w.shape[-1] // 2
    return jnp.concatenate([w[..., half:], w[..., :half]], axis=-1)


def _prep_w_in(w_in):
    o1 = Q_LORA_RANK
    o2 = o1 + KV_LORA_RANK
    o3 = o2 + QK_ROPE_DIM
    o4 = o3 + LRU_WIDTH
    k_pe = w_in[:, o2:o3]
    return jnp.concatenate([w_in[:, :o1], w_in[:, o1:o2], w_in[:, o3:o4], w_in[:, o4:],
                            k_pe, _swap_halves(k_pe)], axis=-1).astype(_BF16)


def _prep_w_uq(w_uq):
    w = w_uq.reshape(Q_LORA_RANK, MLA_HEADS, QK_DIM)
    pe = w[..., QK_NOPE_DIM:]
    w = jnp.concatenate([w[..., :QK_NOPE_DIM], pe, _swap_halves(pe)], axis=-1)
    return w.reshape(Q_LORA_RANK, MLA_HEADS * QK_PAD).astype(_BF16)


def _prep_w_ukv(w_ukv):
    w = w_ukv.reshape(KV_LORA_RANK, MLA_HEADS, QK_NOPE_DIM + V_HEAD_DIM)
    w_k = w[..., :QK_NOPE_DIM].reshape(KV_LORA_RANK, MLA_HEADS * QK_NOPE_DIM)
    w_v = w[..., QK_NOPE_DIM:].reshape(KV_LORA_RANK, MLA_HEADS * V_HEAD_DIM)
    return w_k.astype(_BF16), w_v.T.astype(_BF16)


def _block_diag(w):
    nb, d, e = w.shape
    eye = jnp.eye(nb, dtype=w.dtype)
    return (eye[:, None, :, None] * w[:, :, None, :]).reshape(nb * d, nb * e)


def kernel(x, mem, positions, ffn1_pre_g, ffn1_w_gu, ffn1_w_down, ffn1_post_g, mix_pre_g, w_in, q_a_norm_g, w_uq, kv_a_norm_g, w_ukv, conv_w, conv_b, rg_w_a, rg_b_a, rg_w_x, rg_b_x, rg_lambda, w_out, mix_post_g, xa_pre_g, mem_norm_g, xa_w_q, xa_w_kv, xa_w_o, xa_post_g, ffn2_pre_g, ffn2_w_gu, ffn2_w_down, ffn2_post_g):
    batch, seq, _ = x.shape
    n = batch * seq
    depth = ffn1_pre_g.shape[0]
    bf = lambda w: w.astype(_BF16)
    vec = lambda g: g.reshape(1, -1)

    inv = ROPE_THETA ** (-jnp.arange(0, QK_ROPE_DIM, 2, dtype=_F32) / QK_ROPE_DIM)
    inv = jnp.tile(inv, 2 * LANES // QK_ROPE_DIM).reshape(1, LANES)
    pos = positions.reshape(n, 1)

    xf = x.reshape(n, D_MODEL)
    for l in range(depth):
        ffn2 = (vec(ffn2_pre_g[l]), ffn2_w_gu[l], ffn2_w_down[l], vec(ffn2_post_g[l]))
        w_k, w_vt = _prep_w_ukv(w_ukv[l])
        w_ax = bf(jnp.concatenate([_block_diag(rg_w_a[l]), _block_diag(rg_w_x[l])], axis=-1))
        b_ax = jnp.concatenate([rg_b_a[l].reshape(1, -1), rg_b_x[l].reshape(1, -1)], axis=-1)
        xf = _ffn(xf, vec(ffn1_pre_g[l]), ffn1_w_gu[l], ffn1_w_down[l], vec(ffn1_post_g[l]))
        q, k, vt, y_lru, kmax = _mixin_lru(
            xf, pos, inv, vec(mix_pre_g[l]), _prep_w_in(w_in[l]), vec(q_a_norm_g[l]),
            _prep_w_uq(w_uq[l]), vec(kv_a_norm_g[l]), w_k, w_vt, conv_w[l], vec(conv_b[l]),
            w_ax, b_ax, vec(rg_lambda[l]), batch, seq)
        y_mla = _attention(q, k, vt, kmax, batch, seq).reshape(n, MLA_WIDTH)

        kv = _memkv(mem, vec(mem_norm_g[l]), bf(xa_w_kv[l]))
        xf = _mixout_ffn(xf, y_mla, y_lru, bf(w_out[l]), vec(mix_post_g[l]), vec(xa_pre_g[l]),
                         bf(xa_w_q[l]), kv, bf(xa_w_o[l]), vec(xa_post_g[l]), ffn2, seq)
    return xf.reshape(batch, seq, D_MODEL)
```
